```python
import math
import jax, jax.numpy as jnp
from jax import lax
import numpy as np

D_MODEL = 2048
BATCH = 32
SEQ = 256
DEPTH = 2
DEC_BATCH = 2
DEC_SEQ = 1024
PAST_LEN = 512

GRID_W = 64
N_EVEN = (DEPTH + 1) // 2
N_ODD = DEPTH // 2
MIX_W = D_MODEL
A_HEADS = MIX_W // 256
A_DK = 128
A_DV = 128
A_QK = A_HEADS * A_DK
A_V = A_HEADS * A_DV
GLA_CHUNK = 16
B_W = MIX_W // 2
CONV_K = 31
C_HEADS = MIX_W // 512
C_DH = 128
C_W = C_HEADS * 2 * C_DH
D_HEADS = MIX_W // 256
Q_RANK = D_MODEL // 4
KV_RANK = D_MODEL // 8
NOPE_DIM = 128
ROPE_DIM = 64
V_DIM = 128
D_W = D_HEADS * V_DIM
ROPE_BASE = 10000.0
Q_BLOCK = 128
N_EXPERTS = 32
TOP_K = 4
D_FF = D_MODEL
SWIGLU_ALPHA = 1.702
SWIGLU_LIMIT = 7.0
MOE_BLOCK = 128
EVEN_IN = 3 * A_QK + 2 * A_V + 2 * B_W
ODD_IN = 3 * C_W + Q_RANK + KV_RANK + ROPE_DIM
EPS = 1e-6

kernel_name = 'hybrid_diffusion_prefix_step'


def rms_norm(x, g):
    xf = x.astype(jnp.float32)
    y = xf * lax.rsqrt(jnp.mean(xf * xf, axis=-1, keepdims=True) + EPS)
    return (y * g.astype(jnp.float32)).astype(x.dtype)


def layer_norm(x, g, b):
    xf = x.astype(jnp.float32)
    mu = jnp.mean(xf, axis=-1, keepdims=True)
    xc = xf - mu
    y = xc * lax.rsqrt(jnp.mean(xc * xc, axis=-1, keepdims=True) + EPS)
    return (y * g.astype(jnp.float32) + b.astype(jnp.float32)).astype(x.dtype)


def adaln(cvec, w, b):
    m = jax.nn.silu(cvec) @ w + b
    return jnp.split(m[:, None, :], 6, axis=-1)


def modulate(x, g, shift, scale):
    return rms_norm(x, g) * (1.0 + scale) + shift


def axial_rope_tables(L, rot_dim):
    rows = L // GRID_W
    row = jnp.repeat(jnp.arange(rows), GRID_W).astype(jnp.float32)
    col = jnp.tile(jnp.arange(GRID_W), rows).astype(jnp.float32)
    quarter = rot_dim // 4
    inv = ROPE_BASE ** (-jnp.arange(quarter, dtype=jnp.float32) / quarter)
    ang = jnp.concatenate([row[:, None] * inv, col[:, None] * inv], axis=-1)
    return jnp.cos(ang)[None, :, None, :], jnp.sin(ang)[None, :, None, :]


def apply_rope(x, cos, sin):
    xf = x.astype(jnp.float32)
    x1, x2 = jnp.split(xf, 2, axis=-1)
    return jnp.concatenate([x1 * cos - x2 * sin, x2 * cos + x1 * sin], axis=-1).astype(x.dtype)


def gla_chunkwise(q, k, v, logf, s0):
    bsz, L, H, _ = q.shape
    dv = v.shape[-1]
    n = L // GLA_CHUNK
    blk = lambda t: t.astype(jnp.float32).reshape(bsz, n, GLA_CHUNK, H, t.shape[-1])
    q, k, v, logf = blk(q), blk(k), blk(v), blk(logf)
    b = jnp.cumsum(logf, axis=2)
    b_last = b[:, :, -1]
    causal = jnp.tril(jnp.ones((GLA_CHUNK, GLA_CHUNK), bool))[:, :, None, None]
    rel = b[:, :, :, None] - b[:, :, None, :]
    decay = jnp.exp(jnp.where(causal, rel, -jnp.inf))
    scores = jnp.einsum('bnthd,bnshd,bntshd->bnhts', q, k, decay)
    o_intra = jnp.einsum('bnhts,bnshe->bnthe', scores, v)
    kv = jnp.einsum('bnshd,bnshe->bnhde', k * jnp.exp(b_last[:, :, None] - b), v)

    def step(S, inp):
        dec, kv_n = inp
        return dec[..., None] * S + kv_n, S

    s_final, s_prev = lax.scan(step, s0.astype(jnp.float32),
                               (jnp.moveaxis(jnp.exp(b_last), 1, 0), jnp.moveaxis(kv, 1, 0)))
    o_inter = jnp.einsum('bnthd,nbhde->bnthe', q * jnp.exp(b), s_prev)
    return (o_intra + o_inter).reshape(bsz, L, H, dv), s_final


def hgrn2_direction(q, v, f_pre, lb, s0, reverse):
    f = lb + (1.0 - lb) * jax.nn.sigmoid(f_pre.astype(jnp.float32))
    k, logf = 1.0 - f, jnp.log(f)
    if reverse:
        q, k, v, logf = (jnp.flip(t, 1) for t in (q, k, v, logf))
    o, s = gla_chunkwise(q, k, v, logf, s0)
    return (jnp.flip(o, 1) if reverse else o), s


def depthwise_conv(x, w, b):
    y = lax.conv_general_dilated(x, w[:, None, :], window_strides=(1,),
                                 padding=[(CONV_K // 2, CONV_K // 2)],
                                 dimension_numbers=('NWC', 'WIO', 'NWC'),
                                 feature_group_count=x.shape[-1])
    return y + b


def even_mixer(h, w_in, lb, g_out, conv_w, conv_b, ln_g, ln_b, w_out, s0):
    bsz, L, _ = h.shape
    q, v, f_fw, f_bw, g, u = jnp.split(h @ w_in, np.cumsum([A_QK, A_V, A_QK, A_QK, A_V]).tolist(), axis=-1)
    heads = lambda t, d: t.reshape(bsz, L, A_HEADS, d)
    q = jax.nn.silu(heads(q, A_DK))
    v = heads(v, A_DV)
    o_fw, s_fw = hgrn2_direction(q, v, heads(f_fw, A_DK), lb, s0[:, 0], False)
    o_bw, s_bw = hgrn2_direction(q, v, heads(f_bw, A_DK), lb, s0[:, 1], True)
    o_a = rms_norm(o_fw + o_bw, g_out).reshape(bsz, L, A_V) * jax.nn.silu(g.astype(jnp.float32))
    a, gl = jnp.split(u, 2, axis=-1)
    glu = a * jax.nn.sigmoid(gl)
    o_b = jax.nn.silu(layer_norm(depthwise_conv(glu, conv_w, conv_b), ln_g, ln_b))
    out = jnp.concatenate([o_a.astype(h.dtype), o_b.astype(h.dtype)], axis=-1) @ w_out
    return out.astype(h.dtype), jnp.stack([s_fw, s_bw], axis=1)


def over_query_blocks(fn, qs):
    bsz, Lq = qs[0].shape[:2]
    nb = Lq // Q_BLOCK
    blocks = tuple(jnp.moveaxis(t.reshape(bsz, nb, Q_BLOCK, *t.shape[2:]), 1, 0) for t in qs)
    out = lax.map(fn, blocks)
    return jnp.moveaxis(out, 0, 1).reshape(bsz, Lq, *out.shape[3:])


def segment_probs(q_blocks, k_segs, scale):
    s = jnp.concatenate([jnp.einsum('bqhd,bkhd->bhqk', q, k).astype(jnp.float32)
                         for q, k in zip(q_blocks, k_segs)], axis=-1)
    return jax.nn.softmax(s * scale, axis=-1)


def softmax_attention(q_segs, k_segs, v, scale):
    vf = v.astype(jnp.float32)

    def block(qb):
        return jnp.einsum('bhqk,bkhd->bqhd', segment_probs(qb, k_segs, scale), vf)
    return over_query_blocks(block, tuple(q_segs))


def differential_attention(q1_segs, q2_segs, k1_segs, k2_segs, v, lam, scale):
    vf = v.astype(jnp.float32)
    n = len(q1_segs)

    def block(qb):
        p = segment_probs(qb[:n], k1_segs, scale) - lam * segment_probs(qb[n:], k2_segs, scale)
        return jnp.einsum('bhqk,bkhd->bqhd', p, vf)
    return over_query_blocks(block, tuple(q1_segs) + tuple(q2_segs))


def odd_mixer(h, w_in, w_uq, w_ukv, g_q, g_kv, lam_vec, g_sub, w_out, lam_init, ctx):
    bsz, L, _ = h.shape
    qd, kd, vd, cq, ckv, kr = jnp.split(h @ w_in, np.cumsum([C_W, C_W, C_W, Q_RANK, KV_RANK]).tolist(), axis=-1)
    qd = qd.reshape(bsz, L, C_HEADS, 2 * C_DH)
    kd = kd.reshape(bsz, L, C_HEADS, 2 * C_DH)
    vd = vd.reshape(bsz, L, C_HEADS, 2 * C_DH)
    qm = (rms_norm(cq, g_q) @ w_uq).reshape(bsz, L, D_HEADS, NOPE_DIM + ROPE_DIM)
    ckv = rms_norm(ckv, g_kv)
    own = (kd, vd, ckv, kr)
    kd_c, vd_c, ckv_c, kr_c = own if ctx is None else ctx

    def mla_kv(c_lat, k_rope):
        lead = c_lat.shape[:2]
        kv = (c_lat @ w_ukv).reshape(*lead, D_HEADS, NOPE_DIM + V_DIM)
        k_rope = jnp.broadcast_to(k_rope[:, :, None, :], (*lead, D_HEADS, ROPE_DIM))
        return jnp.concatenate([kv[..., :NOPE_DIM], k_rope], axis=-1), kv[..., NOPE_DIM:]

    q1, q2 = qd[..., :C_DH], qd[..., C_DH:]
    dq1, dq2 = [q1], [q2]
    dk1, dk2 = [kd_c[..., :C_DH]], [kd_c[..., C_DH:]]
    dv = [vd_c]
    mk_c, mv_c = mla_kv(ckv_c, kr_c)
    mq, mk, mv = [qm], [mk_c], [mv_c]
    if ctx is not None:
        cos_c, sin_c = axial_rope_tables(L, C_DH)
        rot = lambda t: apply_rope(t, cos_c, sin_c)
        dq1.append(rot(q1))
        dq2.append(rot(q2))
        dk1.append(rot(kd[..., :C_DH]))
        dk2.append(rot(kd[..., C_DH:]))
        dv.append(vd)
        cos_m, sin_m = axial_rope_tables(L, ROPE_DIM)
        mq.append(jnp.concatenate([qm[..., :NOPE_DIM], apply_rope(qm[..., NOPE_DIM:], cos_m, sin_m)], axis=-1))
        mk_l, mv_l = mla_kv(ckv, apply_rope(kr[:, :, None, :], cos_m, sin_m)[:, :, 0])
        mk.append(mk_l)
        mv.append(mv_l)
    lq = lam_vec.astype(jnp.float32)
    lam = jnp.exp(jnp.sum(lq[0] * lq[1])) - jnp.exp(jnp.sum(lq[2] * lq[3])) + lam_init
    o_c = differential_attention(dq1, dq2, dk1, dk2, jnp.concatenate(dv, axis=1), lam, C_DH ** -0.5)
    o_c = (rms_norm(o_c, g_sub) * (1.0 - lam_init)).reshape(bsz, L, C_W)
    o_d = softmax_attention(mq, mk, jnp.concatenate(mv, axis=1), (NOPE_DIM + ROPE_DIM) ** -0.5).reshape(bsz, L, D_W)
    out = jnp.concatenate([o_c.astype(h.dtype), o_d.astype(h.dtype)], axis=-1) @ w_out
    return out.astype(h.dtype), own


def clamped_swiglu(gate, up):
    gate = jnp.minimum(gate, SWIGLU_LIMIT)
    up = jnp.clip(up, -SWIGLU_LIMIT, SWIGLU_LIMIT)
    return gate * jax.nn.sigmoid(SWIGLU_ALPHA * gate) * (up + 1.0)


def moe(x, w_router, b_router, w_gate, b_gate, w_up, b_up, w_down, b_down):
    shape = x.shape
    xt = x.reshape(-1, shape[-1])
    n = xt.shape[0]
    logits = (xt @ w_router).astype(jnp.float32) + b_router.astype(jnp.float32)
    top_val, top_idx = lax.top_k(logits, TOP_K)
    gates = jax.nn.softmax(top_val, axis=-1).reshape(-1)
    expert = top_idx.reshape(-1)
    token = jnp.repeat(jnp.arange(n), TOP_K)
    order = jnp.argsort(expert)
    expert, token, gates = expert[order], token[order], gates[order]
    counts = jnp.zeros((N_EXPERTS,), jnp.int32).at[expert].add(1)
    padded = (counts + MOE_BLOCK - 1) // MOE_BLOCK * MOE_BLOCK
    start = jnp.cumsum(counts) - counts
    pend = jnp.cumsum(padded)
    dest = (pend - padded)[expert] + jnp.arange(n * TOP_K) - start[expert]
    n_blocks = -(-(n * TOP_K) // MOE_BLOCK) + N_EXPERTS
    x_rows = jnp.zeros((n_blocks * MOE_BLOCK, shape[-1]), x.dtype).at[dest].set(xt[token])
    blk_expert = jnp.minimum(jnp.searchsorted(pend, jnp.arange(n_blocks) * MOE_BLOCK, side='right'), N_EXPERTS - 1)

    def expert_block(args):
        xb, e = args
        hb = clamped_swiglu(xb @ w_gate[e] + b_gate[e], xb @ w_up[e] + b_up[e])
        return hb @ w_down[e] + b_down[e]

    y_rows = lax.map(expert_block, (x_rows.reshape(n_blocks, MOE_BLOCK, shape[-1]), blk_expert))
    y = y_rows.reshape(n_blocks * MOE_BLOCK, shape[-1])[dest].astype(jnp.float32) * gates[:, None]
    out = jnp.zeros((n, shape[-1]), jnp.float32).at[token].add(y)
    return out.astype(x.dtype).reshape(shape)


def setup_inputs(seed: int = 0) -> dict:
    key = jax.random.key(seed)
    keys = iter(jax.random.split(key, 64))
    nrm = lambda shape, scale: jax.random.normal(next(keys), shape, jnp.float32) * scale
    gain = lambda shape: 1.0 + nrm(shape, 0.1)
    D = D_MODEL
    return {
        'x_prompt': nrm((BATCH, SEQ, D), 1.0),
        'x_sample': nrm((DEC_BATCH, DEC_SEQ, D), 1.0),
        'state_hgrn': nrm((DEC_BATCH, N_EVEN, 2, A_HEADS, A_DK, A_DV), 0.5),
        'cache_diff_k': nrm((DEC_BATCH, N_ODD, PAST_LEN, C_HEADS, 2 * C_DH), 1.0),
        'cache_diff_v': nrm((DEC_BATCH, N_ODD, PAST_LEN, C_HEADS, 2 * C_DH), 1.0),
        'cache_mla_ckv': nrm((DEC_BATCH, N_ODD, PAST_LEN, KV_RANK), 1.0),
        'cache_mla_kr': nrm((DEC_BATCH, N_ODD, PAST_LEN, ROPE_DIM), 1.0),
        'c': nrm((DEC_BATCH, D), 1.0),
        'c_ctx': nrm((D,), 1.0),
        'norm_mix': gain((DEPTH, D)),
        'norm_ffn': gain((DEPTH, D)),
        'w_ada': nrm((DEPTH, D, 6 * D), 0.5 * D ** -0.5),
        'b_ada': nrm((DEPTH, 6 * D), 0.02),
        'hgrn_lb': nrm((DEPTH + 1, A_QK), 0.1),
        'w_in_even': nrm((N_EVEN, D, EVEN_IN), D ** -0.5),
        'g_hgrn_out': gain((N_EVEN, A_DV)),
        'conv_w': nrm((N_EVEN, CONV_K, B_W), CONV_K ** -0.5),
        'conv_b': nrm((N_EVEN, B_W), 0.02),
        'conv_ln_g': gain((N_EVEN, B_W)),
        'conv_ln_b': nrm((N_EVEN, B_W), 0.02),
        'w_out_even': nrm((N_EVEN, A_V + B_W, D), (A_V + B_W) ** -0.5),
        'w_in_odd': nrm((N_ODD, D, ODD_IN), D ** -0.5),
        'w_uq': nrm((N_ODD, Q_RANK, D_HEADS * (NOPE_DIM + ROPE_DIM)), Q_RANK ** -0.5),
        'w_ukv': nrm((N_ODD, KV_RANK, D_HEADS * (NOPE_DIM + V_DIM)), KV_RANK ** -0.5),
        'g_q': gain((N_ODD, Q_RANK)),
        'g_kv': gain((N_ODD, KV_RANK)),
        'diff_lambda': nrm((N_ODD, 4, C_DH), 0.1),
        'g_sub': gain((N_ODD, 2 * C_DH)),
        'w_out_odd': nrm((N_ODD, C_W + D_W, D), (C_W + D_W) ** -0.5),
        'w_router': nrm((DEPTH, D, N_EXPERTS), D ** -0.5),
        'b_router': nrm((DEPTH, N_EXPERTS), 0.01),
        'w_gate': nrm((DEPTH, N_EXPERTS, D, D_FF), D ** -0.5),
        'b_gate': nrm((DEPTH, N_EXPERTS, D_FF), 0.01),
        'w_up': nrm((DEPTH, N_EXPERTS, D, D_FF), D ** -0.5),
        'b_up': nrm((DEPTH, N_EXPERTS, D_FF), 0.01),
        'w_down': nrm((DEPTH, N_EXPERTS, D_FF, D), D_FF ** -0.5),
        'b_down': nrm((DEPTH, N_EXPERTS, D), 0.01),
        'norm_final': gain((D,)),
    }


def reference(x_prompt, x_sample, state_hgrn, cache_diff_k, cache_diff_v, cache_mla_ckv, cache_mla_kr, c, c_ctx,
              norm_mix, norm_ffn, w_ada, b_ada, hgrn_lb, w_in_even, g_hgrn_out, conv_w, conv_b, conv_ln_g, conv_ln_b,
              w_out_even, w_in_odd, w_uq, w_ukv, g_q, g_kv, diff_lambda, g_sub, w_out_odd,
              w_router, b_router, w_gate, b_gate, w_up, b_up, w_down, b_down, norm_final):
    xc, xl = x_prompt, x_sample
    lower_bounds = jnp.cumsum(jax.nn.softmax(hgrn_lb.astype(jnp.float32), axis=0), axis=0)
    new_hgrn, new_dk, new_dv, new_ckv, new_kr = [], [], [], [], []
    for l in range(DEPTH):
        j = l // 2
        mc = adaln(c_ctx[None, :], w_ada[l], b_ada[l])
        ml = adaln(c, w_ada[l], b_ada[l])
        hc = modulate(xc, norm_mix[l], mc[0], mc[1])
        hl = modulate(xl, norm_mix[l], ml[0], ml[1])
        if l % 2 == 0:
            ev = (w_in_even[j], lower_bounds[l].reshape(A_HEADS, A_DK), g_hgrn_out[j], conv_w[j], conv_b[j],
                  conv_ln_g[j], conv_ln_b[j], w_out_even[j])
            zeros = jnp.zeros((xc.shape[0], 2, A_HEADS, A_DK, A_DV), jnp.float32)
            oc, s_ctx = even_mixer(hc, *ev, zeros)
            ol, _ = even_mixer(hl, *ev, state_hgrn[:, j])
            new_hgrn.append(s_ctx.astype(x_prompt.dtype))
        else:
            od = (w_in_odd[j], w_uq[j], w_ukv[j], g_q[j], g_kv[j], diff_lambda[j], g_sub[j], w_out_odd[j])
            lam_init = 0.8 - 0.6 * math.exp(-0.3 * l)
            oc, (kd, vd, ckv, kr) = odd_mixer(hc, *od, lam_init, None)
            ol, _ = odd_mixer(hl, *od, lam_init,
                              (cache_diff_k[:, j], cache_diff_v[:, j], cache_mla_ckv[:, j], cache_mla_kr[:, j]))
            new_dk.append(kd)
            new_dv.append(vd)
            new_ckv.append(ckv)
            new_kr.append(kr)
        xc = xc + mc[2] * oc
        xl = xl + ml[2] * ol
        mp = (w_router[l], b_router[l], w_gate[l], b_gate[l], w_up[l], b_up[l], w_down[l], b_down[l])
        xc = xc + mc[5] * moe(modulate(xc, norm_ffn[l], mc[3], mc[4]), *mp)
        xl = xl + ml[5] * moe(modulate(xl, norm_ffn[l], ml[3], ml[4]), *mp)
    y_prompt = rms_norm(xc, norm_final)
    y_sample = rms_norm(xl, norm_final)
    return (y_prompt, y_sample, jnp.stack(new_hgrn, axis=1), jnp.stack(new_dk, axis=1), jnp.stack(new_dv, axis=1),
            jnp.stack(new_ckv, axis=1), jnp.stack(new_kr, axis=1))
```

```python
import functools
import math

import numpy as np
import jax
import jax.numpy as jnp
from jax import lax
from jax.experimental import pallas as pl
from jax.experimental.pallas import tpu as pltpu

F32 = jnp.float32
BF16 = jnp.bfloat16

D = 2048
BATCH, SEQ = 32, 256
DEC_BATCH, DEC_SEQ = 2, 1024
PAST = 512
DEPTH = 2
GRID_W = 64
T_CTX = BATCH * SEQ
T_LAT = DEC_BATCH * DEC_SEQ
T = T_CTX + T_LAT
A_HEADS, A_DK = 8, 128
A_QK = A_HEADS * A_DK
B_W = 1024
CONV_K = 31
C_HEADS, C_DH = 4, 128
C_W = C_HEADS * 2 * C_DH
D_HEADS = 8
Q_RANK, KV_RANK = 512, 256
NOPE, ROPE, V_DIM = 128, 64, 128
ROPE_BASE = 10000.0
N_EXPERTS, TOP_K = 32, 4
D_FF = 2048
SWIGLU_ALPHA, SWIGLU_LIMIT = 1.702, 7.0
EVEN_IN = 3 * A_QK + 2 * A_QK + 2 * B_W
ODD_IN = 3 * C_W + Q_RANK + KV_RANK + ROPE
EPS = 1e-6

LANE = 128
UNIT = 256
ROW_TILE = 256
MM_TM, MM_TN = 512, 1024
MOE_TM = 256
MOE_TN = 1024
MOE_ROWS = T * TOP_K + N_EXPERTS * MOE_TM
MOE_BLOCKS = MOE_ROWS // MOE_TM
GATHER_TB = 256
COMBINE_TB = 128
VMEM_LIMIT = 56 * 1024 * 1024
HGRN_LEVELS = (16, 32, 64, 128, 256)
HGRN_CLAMP = 40.0


def _cparams(*sem):
    return pltpu.CompilerParams(dimension_semantics=sem, vmem_limit_bytes=VMEM_LIMIT)


def _dot(a, b):
    return lax.dot_general(a, b, (((1,), (0,)), ((), ())), preferred_element_type=F32)


def _dot_nt(a, b):
    return lax.dot_general(a, b, (((1,), (1,)), ((), ())), preferred_element_type=F32)


def _dot_tn(a, b):
    return lax.dot_general(a, b, (((0,), (0,)), ((), ())), preferred_element_type=F32)


def _sigmoid(x):
    return 1.0 / (1.0 + jnp.exp(-x))


def _silu(x):
    return x * _sigmoid(x)


def _rms(x, g):
    return x * lax.rsqrt(jnp.mean(x * x, axis=-1, keepdims=True) + EPS) * g


def _seg_of_block(i, rows):
    nctx = T_CTX // rows
    return jnp.where(i < nctx, 0, 1 + (i - nctx) // (DEC_SEQ // rows))


def _ada_kernel(c_ref, w_ref, b_ref, o_ref):
    a = _silu(c_ref[...]).astype(BF16)
    o_ref[...] = _dot(a, w_ref[...].astype(BF16)) + b_ref[...]


def _ada(cvec, w_ada, b_ada):
    tn = 1024
    return pl.pallas_call(
        _ada_kernel,
        grid=(DEPTH, 6 * D // tn),
        in_specs=[pl.BlockSpec((8, D), lambda l, j: (0, 0)),
                  pl.BlockSpec((None, D, tn), lambda l, j: (l, 0, j)),
                  pl.BlockSpec((None, 1, tn), lambda l, j: (l, 0, j))],
        out_specs=pl.BlockSpec((None, 8, tn), lambda l, j: (l, 0, j)),
        out_shape=jax.ShapeDtypeStruct((DEPTH, 8, 6 * D), F32),
        compiler_params=_cparams("arbitrary", "arbitrary"),
        name="ada",
    )(cvec, w_ada, b_ada.reshape(DEPTH, 1, 6 * D))


def _mm_kernel(*refs, rms, emit_norm):
    it = iter(refs)
    x_ref, w_ref = next(it), next(it)
    g_ref = next(it) if rms else None
    o_ref = next(it)
    n_ref = next(it) if emit_norm else None
    wbf = next(it)

    @pl.when(pl.program_id(1) == 0)
    def _():
        wbf[...] = w_ref[...].astype(BF16)

    x = x_ref[...]
    if rms:
        x = _rms(x.astype(F32), g_ref[...])
        if emit_norm:
            n_ref[...] = x
    o_ref[...] = _dot(x.astype(BF16), wbf[...]).astype(o_ref.dtype)


def _mm(x, w, *, out_dtype=F32, tm=MM_TM, tn=MM_TN, g=None, emit_norm=False, name="mm"):
    M, K = x.shape
    N = w.shape[1]
    tn = min(tn, N)
    nj = pl.cdiv(N, tn)
    assert M % tm == 0 and (not emit_norm or nj == 1)
    in_specs = [pl.BlockSpec((tm, K), lambda j, i: (i, 0)),
                pl.BlockSpec((K, tn), lambda j, i: (0, j))]
    args = [x, w]
    if g is not None:
        in_specs.append(pl.BlockSpec((1, K), lambda j, i: (0, 0)))
        args.append(g.reshape(1, K))
    out_specs = [pl.BlockSpec((tm, tn), lambda j, i: (i, j))]
    out_shape = [jax.ShapeDtypeStruct((M, N), out_dtype)]
    if emit_norm:
        out_specs.append(pl.BlockSpec((tm, K), lambda j, i: (i, 0)))
        out_shape.append(jax.ShapeDtypeStruct((M, K), F32))
    res = pl.pallas_call(
        functools.partial(_mm_kernel, rms=g is not None, emit_norm=emit_norm),
        grid=(nj, M // tm),
        in_specs=in_specs, out_specs=out_specs, out_shape=out_shape,
        scratch_shapes=[pltpu.VMEM((K, tn), BF16)],
        compiler_params=_cparams("arbitrary", "arbitrary"),
        name=name,
    )(*args)
    return res if emit_norm else res[0]


def _resid_norm(x, o, mods, g, *, has_resid, gate_row, mode, mod_row):
    if has_resid:
        x = x + mods[gate_row:gate_row + 1, :] * o
    h = _rms(x, g)
    if mode == "mod":
        h = h * (1.0 + mods[mod_row + 1:mod_row + 2, :]) + mods[mod_row:mod_row + 1, :]
    return x, h


def _rmod_kernel(*refs, has_resid, gate_row, mode, mod_row, emit_logits):
    it = iter(refs)
    x_ref = next(it)
    o_ref = next(it) if has_resid else None
    mods_ref, g_ref = next(it), next(it)
    if emit_logits:
        wr_ref, br_ref = next(it), next(it)
    x1_ref = next(it) if has_resid else None
    h_ref = next(it)
    x, h = _resid_norm(x_ref[...], o_ref[...] if has_resid else None, mods_ref[...], g_ref[...],
                       has_resid=has_resid, gate_row=gate_row, mode=mode, mod_row=mod_row)
    if has_resid:
        x1_ref[...] = x
    h_ref[...] = h.astype(h_ref.dtype)
    if emit_logits:
        lg_ref = next(it)
        lg_ref[...] = lax.dot_general(h, wr_ref[...], (((1,), (0,)), ((), ())),
                                      precision=lax.Precision.HIGHEST,
                                      preferred_element_type=F32) + br_ref[...]


def _rmod(x, o, mods, g, *, gate_row=0, mode="mod", mod_row=0, h_dtype=BF16, router=None):
    tm = ROW_TILE
    has_resid = o is not None
    row = pl.BlockSpec((tm, D), lambda i: (i, 0))
    in_specs, args = [row], [x]
    if has_resid:
        in_specs.append(row)
        args.append(o)
    in_specs += [pl.BlockSpec((None, 8, D), lambda i: (_seg_of_block(i, tm), 0, 0)),
                 pl.BlockSpec((1, D), lambda i: (0, 0))]
    args += [mods, g.reshape(1, D)]
    if router is not None:
        w_router, b_router, l = router
        in_specs += [pl.BlockSpec((None, D, N_EXPERTS), lambda i: (l, 0, 0)),
                     pl.BlockSpec((None, 1, N_EXPERTS), lambda i: (l, 0, 0))]
        args += [w_router, b_router.reshape(DEPTH, 1, N_EXPERTS)]
    out_specs, out_shape = [], []
    if has_resid:
        out_specs.append(row)
        out_shape.append(jax.ShapeDtypeStruct((T, D), F32))
    out_specs.append(row)
    out_shape.append(jax.ShapeDtypeStruct((T, D), h_dtype))
    if router is not None:
        out_specs.append(pl.BlockSpec((tm, N_EXPERTS), lambda i: (i, 0)))
        out_shape.append(jax.ShapeDtypeStruct((T, N_EXPERTS), F32))
    return pl.pallas_call(
        functools.partial(_rmod_kernel, has_resid=has_resid, gate_row=gate_row, mode=mode,
                          mod_row=mod_row, emit_logits=router is not None),
        grid=(T // tm,), in_specs=in_specs, out_specs=out_specs, out_shape=out_shape,
        compiler_params=_cparams("arbitrary"),
        name="rmod",
    )(*args)


def _hgrn_consts(rev):
    L = UNIT
    t = np.arange(L)[:, None]
    s = np.arange(L)[None, :]
    order = (s >= t) if rev else (s <= t)
    masks = [((t // HGRN_LEVELS[0]) == (s // HGRN_LEVELS[0])) & order]
    masks += [(t // b) == (s // b) for b in HGRN_LEVELS[1:]]
    return jnp.asarray(order, BF16), jnp.asarray(np.stack(masks), F32)


def _hgrn_kernel(q_ref, v_ref, f_ref, lb_ref, s0_ref, tri_ref, msk_ref, o_ref, so_ref, st, *, rev):
    L = UNIT
    n = pl.program_id(1)
    is_ctx = n < BATCH

    @pl.when(is_ctx)
    def _():
        st[...] = jnp.zeros_like(st)

    @pl.when(jnp.logical_and(n >= BATCH, (n - BATCH) % (DEC_SEQ // L) == 0))
    def _():
        st[...] = s0_ref[...].T

    q = _silu(q_ref[...])
    v = v_ref[...].astype(BF16)
    lb = lb_ref[...]
    f = lb + (1.0 - lb) * _sigmoid(f_ref[...])
    k = 1.0 - f
    logf = jnp.log(f)
    hi = logf.astype(BF16)
    r1 = logf - hi.astype(F32)
    mid = r1.astype(BF16)
    lo = (r1 - mid.astype(F32)).astype(BF16)
    tri = tri_ref[...]
    bcum = _dot(tri, hi) + _dot(tri, mid) + _dot(tri, lo)

    rowi = lax.broadcasted_iota(jnp.int32, (L, A_DK), 0)
    scores = jnp.zeros((L, L), F32)
    for li, b in enumerate(HGRN_LEVELS):
        r = b // 2 if (rev or li == 0) else b // 2 - 1
        b3 = bcum.reshape(L // b, b, A_DK)
        ref = jnp.broadcast_to(b3[:, r:r + 1, :], (L // b, b, A_DK)).reshape(L, A_DK)
        dlt = bcum - ref
        if li == 0:
            qt = q * jnp.exp(jnp.clip(dlt, -HGRN_CLAMP, HGRN_CLAMP))
            kt = k * jnp.exp(jnp.clip(-dlt, -HGRN_CLAMP, HGRN_CLAMP))
        else:
            later = (rowi % b) >= (b // 2)
            q_rows = jnp.logical_not(later) if rev else later
            e = jnp.exp(-jnp.abs(dlt))
            qt = jnp.where(q_rows, q * e, 0.0)
            kt = jnp.where(q_rows, 0.0, k * e)
        scores = scores + msk_ref[li] * _dot_nt(qt.astype(BF16), kt.astype(BF16))

    s_prev = st[...]
    o = _dot(scores.astype(BF16), v) + _dot_nt((q * jnp.exp(bcum)).astype(BF16), s_prev.astype(BF16))
    o_ref[...] = o
    b_end = bcum[0:1, :] if rev else bcum[L - 1:L, :]
    kk = (k * jnp.exp(b_end - bcum)).astype(BF16)
    s_new = s_prev * jnp.exp(b_end) + _dot_tn(v, kk)
    st[...] = s_new

    @pl.when(is_ctx)
    def _():
        so_ref[...] = s_new.T


def _hgrn_dir(y0, lb, s0, *, rev):
    L = UNIT
    per = DEC_SEQ // L
    nb = A_QK // LANE
    d = 1 if rev else 0
    tri, msk = _hgrn_consts(rev)

    def rb(n):
        if not rev:
            return n
        m = n - BATCH
        return jnp.where(n < BATCH, n, BATCH + (m // per) * per + (per - 1 - m % per))

    def s0_idx(h, n):
        b = jnp.clip((n - BATCH) // per, 0, DEC_BATCH - 1)
        return (b * 2 * A_HEADS + d * A_HEADS + h, 0, 0)

    o, so = pl.pallas_call(
        functools.partial(_hgrn_kernel, rev=rev),
        grid=(A_HEADS, T // L),
        in_specs=[pl.BlockSpec((L, LANE), lambda h, n: (rb(n), h)),
                  pl.BlockSpec((L, LANE), lambda h, n: (rb(n), nb + h)),
                  pl.BlockSpec((L, LANE), lambda h, n: (rb(n), (2 + d) * nb + h)),
                  pl.BlockSpec((1, LANE), lambda h, n: (0, h)),
                  pl.BlockSpec((None, A_DK, A_DK), s0_idx),
                  pl.BlockSpec((L, L), lambda h, n: (0, 0)),
                  pl.BlockSpec((len(HGRN_LEVELS), L, L), lambda h, n: (0, 0, 0))],
        out_specs=[pl.BlockSpec((L, LANE), lambda h, n: (rb(n), h)),
                   pl.BlockSpec((None, A_DK, A_DK),
                                lambda h, n: (jnp.minimum(n, BATCH - 1) * A_HEADS + h, 0, 0))],
        out_shape=[jax.ShapeDtypeStruct((T, A_QK), F32),
                   jax.ShapeDtypeStruct((BATCH * A_HEADS, A_DK, A_DK), F32)],
        scratch_shapes=[pltpu.VMEM((A_DK, A_DK), F32)],
        compiler_params=_cparams("arbitrary", "arbitrary"),
        name="hgrn_bwd" if rev else "hgrn_fwd",
    )(y0, y0, y0, lb, s0, tri, msk)
    return o, so.reshape(BATCH, A_HEADS, A_DK, A_DK)


def _hgrn_out_kernel(of_ref, ob_ref, g_ref, gn_ref, o_ref):
    o = _rms(of_ref[...] + ob_ref[...], gn_ref[...])
    o_ref[...] = (o * _silu(g_ref[...])).astype(o_ref.dtype)


def _hgrn_out(o_fw, o_bw, y0, g_out):
    tm = 1024
    nb = A_QK // LANE
    blk = lambda off: pl.BlockSpec((tm, LANE), lambda i, h: (i, off + h))
    return pl.pallas_call(
        _hgrn_out_kernel,
        grid=(T // tm, A_HEADS),
        in_specs=[blk(0), blk(0), blk(4 * nb), pl.BlockSpec((1, LANE), lambda i, h: (0, 0))],
        out_specs=blk(0),
        out_shape=jax.ShapeDtypeStruct((T, A_QK), BF16),
        compiler_params=_cparams("arbitrary", "arbitrary"),
        name="hgrn_out",
    )(o_fw, o_bw, y0, g_out.reshape(1, LANE))


CONV_HALO = 16
CONV_RC, CONV_CC = 32, 256


def _conv_kernel(ap_ref, gp_ref, a_ref, g_ref, an_ref, gn_ref, w_ref, b_ref, lg_ref, lb_ref,
                 o_ref, pad, yb):
    L = UNIT
    n = pl.program_id(0)
    s = (n - BATCH) % (DEC_SEQ // L)
    lat = n >= BATCH
    has_prev = jnp.logical_and(lat, s > 0).astype(F32)
    has_next = jnp.logical_and(lat, s < DEC_SEQ // L - 1).astype(F32)
    glu = lambda a, g: a[...] * _sigmoid(g[...])
    pad[0:CONV_HALO, :] = glu(ap_ref, gp_ref) * has_prev
    pad[CONV_HALO:CONV_HALO + L, :] = glu(a_ref, g_ref)
    pad[CONV_HALO + L:CONV_HALO + L + CONV_HALO, :] = glu(an_ref, gn_ref) * has_next
    base = CONV_HALO - CONV_K // 2
    for c0 in range(0, B_W, CONV_CC):
        w = w_ref[:, c0:c0 + CONV_CC]
        for r0 in range(0, L, CONV_RC):
            acc = jnp.zeros((CONV_RC, CONV_CC), F32)
            for kk in range(CONV_K):
                acc = acc + w[kk:kk + 1, :] * pad[r0 + base + kk:r0 + base + kk + CONV_RC, c0:c0 + CONV_CC]
            yb[r0:r0 + CONV_RC, c0:c0 + CONV_CC] = acc
    y = yb[...] + b_ref[...]
    mu = jnp.mean(y, axis=-1, keepdims=True)
    yc = y - mu
    yn = yc * lax.rsqrt(jnp.mean(yc * yc, axis=-1, keepdims=True) + EPS) * lg_ref[...] + lb_ref[...]
    o_ref[...] = _silu(yn).astype(o_ref.dtype)


def _conv(y0, conv_w, conv_b, ln_g, ln_b):
    L = UNIT
    ca = (3 * A_QK + 2 * A_QK) // B_W
    hb = L // CONV_HALO
    nhalo = T // CONV_HALO
    prev = lambda off: pl.BlockSpec((CONV_HALO, B_W), lambda n: (jnp.maximum(n * hb - 1, 0), off))
    cur = lambda off: pl.BlockSpec((L, B_W), lambda n: (n, off))
    nxt = lambda off: pl.BlockSpec((CONV_HALO, B_W), lambda n: (jnp.minimum((n + 1) * hb, nhalo - 1), off))
    vec = pl.BlockSpec((1, B_W), lambda n: (0, 0))
    return pl.pallas_call(
        _conv_kernel,
        grid=(T // L,),
        in_specs=[prev(ca), prev(ca + 1), cur(ca), cur(ca + 1), nxt(ca), nxt(ca + 1),
                  pl.BlockSpec((CONV_K, B_W), lambda n: (0, 0)), vec, vec, vec],
        out_specs=pl.BlockSpec((L, B_W), lambda n: (n, 0)),
        out_shape=jax.ShapeDtypeStruct((T, B_W), BF16),
        scratch_shapes=[pltpu.VMEM((L + 2 * CONV_HALO, B_W), F32), pltpu.VMEM((L, B_W), F32)],
        compiler_params=_cparams("arbitrary"),
        name="conv",
    )(y0, y0, y0, y0, y0, y0, conv_w, conv_b.reshape(1, B_W), ln_g.reshape(1, B_W), ln_b.reshape(1, B_W))


def _axial_tables(L, rot_dim):
    rows = L // GRID_W
    row = np.repeat(np.arange(rows), GRID_W).astype(np.float32)
    col = np.tile(np.arange(GRID_W), rows).astype(np.float32)
    quarter = rot_dim // 4
    inv = (ROPE_BASE ** (-np.arange(quarter, dtype=np.float32) / quarter)).astype(np.float32)
    ang = np.concatenate([row[:, None] * inv, col[:, None] * inv], axis=-1)
    return np.cos(ang).astype(np.float32), np.sin(ang).astype(np.float32)


def _rope_tables(L, rot_dim, reps):
    cos, sin = _axial_tables(L, rot_dim)
    c = np.tile(np.concatenate([cos, cos], axis=-1), (1, reps))
    s = np.tile(np.concatenate([-sin, sin], axis=-1), (1, reps))
    return jnp.asarray(c), jnp.asarray(s)


def _rope(x, c, s, rot_dim):
    w = x.shape[-1]
    half = rot_dim // 2
    if rot_dim == w:
        swapped = pltpu.roll(x, half, 1)
    else:
        lane = lax.broadcasted_iota(jnp.int32, x.shape, 1)
        swapped = jnp.where((lane % rot_dim) < half, pltpu.roll(x, w - half, 1), pltpu.roll(x, half, 1))
    return x * c + swapped * s


def _softmax_parts(scores):
    m = scores[0].max(axis=-1, keepdims=True)
    for s in scores[1:]:
        m = jnp.maximum(m, s.max(axis=-1, keepdims=True))
    es = [jnp.exp(s - m) for s in scores]
    den = es[0].sum(axis=-1, keepdims=True)
    for e in es[1:]:
        den = den + e.sum(axis=-1, keepdims=True)
    inv = 1.0 / den
    return [e * inv for e in es]


def _diff_kernel(*refs, two_seg, lam_init):
    it = iter(refs)
    lam_ref, q1_ref, q2_ref = next(it), next(it), next(it)
    if two_seg:
        k1c_ref, k2c_ref, vc_ref = next(it), next(it), next(it)
    k1_ref, k2_ref, v_ref = next(it), next(it), next(it)
    if two_seg:
        cq_ref, sq_ref, ck_ref, sk_ref = next(it), next(it), next(it), next(it)
    gs_ref, o_ref = next(it), next(it)
    scale = C_DH ** -0.5
    lam = lam_ref[0, 0]
    outs = []
    ps = []
    for q_ref, kc_ref, k_ref in ((q1_ref, k1c_ref if two_seg else None, k1_ref),
                                 (q2_ref, k2c_ref if two_seg else None, k2_ref)):
        q = q_ref[...]
        k = k_ref[...]
        if two_seg:
            sc = [_dot_nt(q.astype(BF16), kc_ref[...].astype(BF16)) * scale,
                  _dot_nt(_rope(q, cq_ref[...], sq_ref[...], C_DH).astype(BF16),
                          _rope(k, ck_ref[...], sk_ref[...], C_DH).astype(BF16)) * scale]
        else:
            sc = [_dot_nt(q.astype(BF16), k.astype(BF16)) * scale]
        ps.append(_softmax_parts(sc))
    vs = ([vc_ref[...]] if two_seg else []) + [v_ref[...]]
    o = None
    for p1, p2, vv in zip(ps[0], ps[1], vs):
        t = _dot((p1 - lam * p2).astype(BF16), vv.astype(BF16))
        o = t if o is None else o + t
    o_ref[...] = (_rms(o, gs_ref[...]) * (1.0 - lam_init)).astype(o_ref.dtype)


def _diff_attn(y1, lam, g_sub, lam_init, cache_k=None, cache_v=None):
    two_seg = cache_k is not None
    hw = 2 * C_DH
    smem = pl.BlockSpec(memory_space=pltpu.SMEM)
    if not two_seg:
        grid = (BATCH, C_HEADS)
        blk = lambda off, w: pl.BlockSpec((SEQ, w), lambda b, h: (b, off + h * (hw // w)))
        in_specs = [smem, blk(0, C_DH), blk(1, C_DH),
                    blk(C_W // C_DH, C_DH), blk(C_W // C_DH + 1, C_DH), blk(2 * C_W // hw, hw),
                    pl.BlockSpec((1, hw), lambda b, h: (0, 0))]
        args = [lam, y1, y1, y1, y1, y1, g_sub.reshape(1, hw)]
        out_specs = pl.BlockSpec((SEQ, hw), lambda b, h: (b, h))
        out_rows = T_CTX
        sem = ("arbitrary", "arbitrary")
    else:
        tq = 256
        nq = DEC_SEQ // tq
        r0 = T_CTX // DEC_SEQ
        grid = (DEC_BATCH, C_HEADS, nq)
        qblk = lambda off: pl.BlockSpec((tq, C_DH), lambda b, h, i: (T_CTX // tq + b * nq + i, off + 2 * h))
        kblk = lambda off, w: pl.BlockSpec((DEC_SEQ, w), lambda b, h, i: (r0 + b, off + h * (hw // w)))
        cblk = lambda off, w: pl.BlockSpec((PAST, w), lambda b, h, i: (b, off + h * (hw // w)))
        tq_blk = pl.BlockSpec((tq, C_DH), lambda b, h, i: (i, 0))
        tk_blk = pl.BlockSpec((DEC_SEQ, C_DH), lambda b, h, i: (0, 0))
        c, s = _rope_tables(DEC_SEQ, C_DH, 1)
        in_specs = [smem, qblk(0), qblk(1), cblk(0, C_DH), cblk(1, C_DH), cblk(0, hw),
                    kblk(C_W // C_DH, C_DH), kblk(C_W // C_DH + 1, C_DH), kblk(2 * C_W // hw, hw),
                    tq_blk, tq_blk, tk_blk, tk_blk,
                    pl.BlockSpec((1, hw), lambda b, h, i: (0, 0))]
        args = [lam, y1, y1, cache_k, cache_k, cache_v, y1, y1, y1, c, s, c, s, g_sub.reshape(1, hw)]
        out_specs = pl.BlockSpec((tq, hw), lambda b, h, i: (b * nq + i, h))
        out_rows = T_LAT
        sem = ("arbitrary", "arbitrary", "arbitrary")
    return pl.pallas_call(
        functools.partial(_diff_kernel, two_seg=two_seg, lam_init=lam_init),
        grid=grid, in_specs=in_specs, out_specs=out_specs,
        out_shape=jax.ShapeDtypeStruct((out_rows, C_W), BF16),
        compiler_params=_cparams(*sem),
        name="diff_lat" if two_seg else "diff_ctx",
    )(*args)


def _mla_kernel(*refs, two_seg):
    it = iter(refs)
    qn_ref, qr_ref = next(it), next(it)
    if two_seg:
        kvc_ref, krc_ref = next(it), next(it)
    kv_ref, kr_ref = next(it), next(it)
    if two_seg:
        cq_ref, sq_ref, ck_ref, sk_ref = next(it), next(it), next(it), next(it)
    o_ref = next(it)
    scale = (NOPE + ROPE) ** -0.5
    qr = qr_ref[...]
    kr = kr_ref[...]
    if two_seg:
        qr_rot = _rope(qr, cq_ref[...], sq_ref[...], ROPE)
        kr_rot = _rope(kr, ck_ref[...], sk_ref[...], ROPE)
        krc = krc_ref[...]
    lane = lax.broadcasted_iota(jnp.int32, (1, LANE), 1)
    for h in range(D_HEADS):
        half = (lane < ROPE) if h % 2 == 0 else (lane >= ROPE)
        pr = slice((h // 2) * LANE, (h // 2 + 1) * LANE)
        qn = qn_ref[:, h * NOPE:(h + 1) * NOPE]
        kn = kv_ref[:, h * 2 * NOPE:h * 2 * NOPE + NOPE]
        vv = kv_ref[:, h * 2 * NOPE + NOPE:(h + 1) * 2 * NOPE]
        cat = lambda a, b: jnp.concatenate([a.astype(BF16), b.astype(BF16)], axis=-1)
        if two_seg:
            knc = kvc_ref[:, h * 2 * NOPE:h * 2 * NOPE + NOPE]
            vc = kvc_ref[:, h * 2 * NOPE + NOPE:(h + 1) * 2 * NOPE]
            sc = [_dot_nt(cat(qn, qr[:, pr]), cat(knc, jnp.where(half, krc, 0.0))) * scale,
                  _dot_nt(cat(qn, qr_rot[:, pr]), cat(kn, jnp.where(half, kr_rot, 0.0))) * scale]
            p = _softmax_parts(sc)
            o = _dot(p[0].astype(BF16), vc) + _dot(p[1].astype(BF16), vv)
        else:
            sc = [_dot_nt(cat(qn, qr[:, pr]), cat(kn, jnp.where(half, kr, 0.0))) * scale]
            o = _dot(_softmax_parts(sc)[0].astype(BF16), vv)
        o_ref[:, h * V_DIM:(h + 1) * V_DIM] = o.astype(o_ref.dtype)


def _mla_attn(qm, kv, y1, kv_c=None, kr_c=None):
    two_seg = kv_c is not None
    qn_w, qr_w = D_HEADS * NOPE, D_HEADS * ROPE
    kr_col = (ODD_IN - ROPE) // LANE
    if not two_seg:
        grid = (BATCH,)
        in_specs = [pl.BlockSpec((SEQ, qn_w), lambda b: (b, 0)),
                    pl.BlockSpec((SEQ, qr_w), lambda b: (b, qn_w // qr_w)),
                    pl.BlockSpec((SEQ, 2 * qn_w), lambda b: (b, 0)),
                    pl.BlockSpec((SEQ, LANE), lambda b: (b, kr_col))]
        args = [qm, qm, kv, y1]
        out_specs = pl.BlockSpec((SEQ, qn_w), lambda b: (b, 0))
        out_rows = T_CTX
        sem = ("arbitrary",)
    else:
        tq = 256
        nq = DEC_SEQ // tq
        r0 = T_CTX // DEC_SEQ
        grid = (DEC_BATCH, nq)
        cq, sq = _rope_tables(DEC_SEQ, ROPE, qr_w // ROPE)
        ck, sk = _rope_tables(DEC_SEQ, ROPE, LANE // ROPE)
        in_specs = [pl.BlockSpec((tq, qn_w), lambda b, i: (T_CTX // tq + b * nq + i, 0)),
                    pl.BlockSpec((tq, qr_w), lambda b, i: (T_CTX // tq + b * nq + i, qn_w // qr_w)),
                    pl.BlockSpec((PAST, 2 * qn_w), lambda b, i: (b, 0)),
                    pl.BlockSpec((PAST, LANE), lambda b, i: (b, 0)),
                    pl.BlockSpec((DEC_SEQ, 2 * qn_w), lambda b, i: (r0 + b, 0)),
                    pl.BlockSpec((DEC_SEQ, LANE), lambda b, i: (r0 + b, kr_col)),
                    pl.BlockSpec((tq, qr_w), lambda b, i: (i, 0)),
                    pl.BlockSpec((tq, qr_w), lambda b, i: (i, 0)),
                    pl.BlockSpec((DEC_SEQ, LANE), lambda b, i: (0, 0)),
                    pl.BlockSpec((DEC_SEQ, LANE), lambda b, i: (0, 0))]
        args = [qm, qm, kv_c, kr_c, kv, y1, cq, sq, ck, sk]
        out_specs = pl.BlockSpec((tq, qn_w), lambda b, i: (b * nq + i, 0))
        out_rows = T_LAT
        sem = ("arbitrary", "arbitrary")
    return pl.pallas_call(
        functools.partial(_mla_kernel, two_seg=two_seg),
        grid=grid, in_specs=in_specs, out_specs=out_specs,
        out_shape=jax.ShapeDtypeStruct((out_rows, qn_w), BF16),
        compiler_params=_cparams(*sem),
        name="mla_lat" if two_seg else "mla_ctx",
    )(*args)


def _gather_kernel(idx_ref, idxn_ref, src_ref, o_ref, buf, sem, *, tb, nblk):
    i = pl.program_id(0)

    def issue(iref, slot):
        def body(r, c):
            pltpu.make_async_copy(src_ref.at[pl.ds(iref[0, 0, r], 1), :],
                                  buf.at[slot, pl.ds(r, 1), :], sem.at[slot]).start()
            return c
        lax.fori_loop(0, tb, body, 0)

    @pl.when(i == 0)
    def _():
        issue(idx_ref, 0)

    @pl.when(i + 1 < nblk)
    def _():
        issue(idxn_ref, (i + 1) % 2)

    slot = i % 2
    pltpu.make_async_copy(src_ref.at[pl.ds(0, tb), :], buf.at[slot], sem.at[slot]).wait()
    o_ref[...] = buf[slot].astype(o_ref.dtype)


def _gather_rows(src, idx, n_rows):
    tb = GATHER_TB
    nblk = n_rows // tb
    idx3 = idx.reshape(nblk, 1, tb)
    return pl.pallas_call(
        functools.partial(_gather_kernel, tb=tb, nblk=nblk),
        grid=(nblk,),
        in_specs=[pl.BlockSpec((1, 1, tb), lambda i: (i, 0, 0), memory_space=pltpu.SMEM),
                  pl.BlockSpec((1, 1, tb), lambda i: (jnp.minimum(i + 1, nblk - 1), 0, 0),
                               memory_space=pltpu.SMEM),
                  pl.BlockSpec(memory_space=pl.ANY)],
        out_specs=pl.BlockSpec((tb, D), lambda i: (i, 0)),
        out_shape=jax.ShapeDtypeStruct((n_rows, D), BF16),
        scratch_shapes=[pltpu.VMEM((2, tb, D), F32), pltpu.SemaphoreType.DMA((2,))],
        compiler_params=_cparams("arbitrary"),
        name="moe_gather",
    )(idx3, idx3, src)


def _swiglu(gate, up):
    gate = jnp.minimum(gate, SWIGLU_LIMIT)
    up = jnp.clip(up, -SWIGLU_LIMIT, SWIGLU_LIMIT)
    return gate * _sigmoid(SWIGLU_ALPHA * gate) * (up + 1.0)


def _new_expert(be_ref, i):
    return jnp.logical_or(i == 0, be_ref[i] != be_ref[jnp.maximum(i - 1, 0)])


def _moe_up_kernel(be_ref, nu_ref, x_ref, wg_ref, wu_ref, bg_ref, bu_ref, o_ref, wg_bf, wu_bf):
    i = pl.program_id(1)

    @pl.when(i < nu_ref[0])
    def _():
        @pl.when(_new_expert(be_ref, i))
        def _():
            wg_bf[...] = wg_ref[...].astype(BF16)
            wu_bf[...] = wu_ref[...].astype(BF16)

        x = x_ref[...]
        gate = _dot(x, wg_bf[...]) + bg_ref[...]
        up = _dot(x, wu_bf[...]) + bu_ref[...]
        o_ref[...] = _swiglu(gate, up).astype(o_ref.dtype)

    @pl.when(i >= nu_ref[0])
    def _():
        o_ref[...] = jnp.zeros_like(o_ref)


def _moe_down_kernel(be_ref, nu_ref, h_ref, wd_ref, bd_ref, o_ref, wd_bf):
    i = pl.program_id(1)

    @pl.when(i < nu_ref[0])
    def _():
        @pl.when(_new_expert(be_ref, i))
        def _():
            wd_bf[...] = wd_ref[...].astype(BF16)

        o_ref[...] = _dot(h_ref[...], wd_bf[...]) + bd_ref[...]

    @pl.when(i >= nu_ref[0])
    def _():
        o_ref[...] = jnp.zeros_like(o_ref)


def _moe_experts(xs, blk_expert, n_used, l, w_gate, b_gate, w_up, b_up, w_down, b_down):
    tm, tn = MOE_TM, MOE_TN
    row = lambda j, i, be, nu: jnp.minimum(i, nu[0] - 1)
    exp = lambda j, i, be, nu: be[jnp.minimum(i, nu[0] - 1)]
    wspec = lambda: pl.BlockSpec((None, None, D, tn), lambda j, i, be, nu: (l, exp(j, i, be, nu), 0, j))
    bspec = lambda: pl.BlockSpec((None, None, 1, tn), lambda j, i, be, nu: (l, exp(j, i, be, nu), 0, j))
    xspec = pl.BlockSpec((tm, D), lambda j, i, be, nu: (row(j, i, be, nu), 0))
    ospec = pl.BlockSpec((tm, tn), lambda j, i, be, nu: (i, j))
    b4 = lambda b: b.reshape(DEPTH, N_EXPERTS, 1, -1)
    hmid = pl.pallas_call(
        _moe_up_kernel,
        grid_spec=pltpu.PrefetchScalarGridSpec(
            num_scalar_prefetch=2, grid=(D_FF // tn, MOE_BLOCKS),
            in_specs=[xspec, wspec(), wspec(), bspec(), bspec()],
            out_specs=ospec,
            scratch_shapes=[pltpu.VMEM((D, tn), BF16), pltpu.VMEM((D, tn), BF16)]),
        out_shape=jax.ShapeDtypeStruct((MOE_ROWS, D_FF), BF16),
        compiler_params=_cparams("arbitrary", "arbitrary"),
        name="moe_up",
    )(blk_expert, n_used, xs, w_gate, w_up, b4(b_gate), b4(b_up))
    return pl.pallas_call(
        _moe_down_kernel,
        grid_spec=pltpu.PrefetchScalarGridSpec(
            num_scalar_prefetch=2, grid=(D // tn, MOE_BLOCKS),
            in_specs=[xspec, wspec(), bspec()],
            out_specs=ospec,
            scratch_shapes=[pltpu.VMEM((D_FF, tn), BF16)]),
        out_shape=jax.ShapeDtypeStruct((MOE_ROWS, D), F32),
        compiler_params=_cparams("arbitrary", "arbitrary"),
        name="moe_down",
    )(blk_expert, n_used, hmid, w_down, b4(b_down))


def _combine_kernel(idx_ref, idxn_ref, y_ref, gates_ref, x_ref, mods_ref, g_ref, *rest,
                    tb, nblk, mode, mod_row, emit_x):
    if emit_x:
        x2_ref, h_ref, buf, sem = rest
    else:
        h_ref, buf, sem = rest
    i = pl.program_id(0)

    def issue(iref, slot):
        def body(r, c):
            for k in range(TOP_K):
                pltpu.make_async_copy(y_ref.at[pl.ds(iref[0, 0, k * tb + r], 1), :],
                                      buf.at[slot, k, pl.ds(r, 1), :], sem.at[slot]).start()
            return c
        lax.fori_loop(0, tb, body, 0)

    @pl.when(i == 0)
    def _():
        issue(idx_ref, 0)

    @pl.when(i + 1 < nblk)
    def _():
        issue(idxn_ref, (i + 1) % 2)

    slot = i % 2
    for k in range(TOP_K):
        pltpu.make_async_copy(y_ref.at[pl.ds(0, tb), :], buf.at[slot, k], sem.at[slot]).wait()
    gates = gates_ref[...]
    moe = gates[:, 0:1] * buf[slot, 0]
    for k in range(1, TOP_K):
        moe = moe + gates[:, k:k + 1] * buf[slot, k]
    x, h = _resid_norm(x_ref[...], moe, mods_ref[...], g_ref[...],
                       has_resid=True, gate_row=5, mode=mode, mod_row=mod_row)
    if emit_x:
        x2_ref[...] = x
    h_ref[...] = h.astype(h_ref.dtype)


def _moe_combine(y, dest, gates, x, mods_l, g, *, mode, mods_next=None, h_dtype=BF16):
    tb = COMBINE_TB
    nblk = T // tb
    idx3 = dest.reshape(nblk, tb, TOP_K).transpose(0, 2, 1).reshape(nblk, 1, TOP_K * tb)
    emit_x = mode == "mod"
    if emit_x:
        mods = jnp.concatenate([mods_next[:, 0:2], mods_l[:, 2:]], axis=1)
    else:
        mods = mods_l
    row = pl.BlockSpec((tb, D), lambda i: (i, 0))
    out_specs = ([row] if emit_x else []) + [row]
    out_shape = ([jax.ShapeDtypeStruct((T, D), F32)] if emit_x else []) + [jax.ShapeDtypeStruct((T, D), h_dtype)]
    return pl.pallas_call(
        functools.partial(_combine_kernel, tb=tb, nblk=nblk, mode=mode, mod_row=0, emit_x=emit_x),
        grid=(nblk,),
        in_specs=[pl.BlockSpec((1, 1, TOP_K * tb), lambda i: (i, 0, 0), memory_space=pltpu.SMEM),
                  pl.BlockSpec((1, 1, TOP_K * tb), lambda i: (jnp.minimum(i + 1, nblk - 1), 0, 0),
                               memory_space=pltpu.SMEM),
                  pl.BlockSpec(memory_space=pl.ANY),
                  pl.BlockSpec((tb, TOP_K), lambda i: (i, 0)),
                  row,
                  pl.BlockSpec((None, 8, D), lambda i: (_seg_of_block(i, tb), 0, 0)),
                  pl.BlockSpec((1, D), lambda i: (0, 0))],
        out_specs=out_specs, out_shape=out_shape,
        scratch_shapes=[pltpu.VMEM((2, TOP_K, tb, D), F32), pltpu.SemaphoreType.DMA((2,))],
        compiler_params=_cparams("arbitrary"),
        name="moe_combine",
    )(idx3, idx3, y, gates, x, mods, g.reshape(1, D))


def _route(logits):
    top_val, top_idx = lax.top_k(logits, TOP_K)
    gates = jax.nn.softmax(top_val, axis=-1)
    expert = top_idx.reshape(-1)
    onehot = (expert[:, None] == jnp.arange(N_EXPERTS)[None, :]).astype(jnp.int32)
    rank = jnp.sum((jnp.cumsum(onehot, axis=0) - 1) * onehot, axis=1)
    counts = jnp.sum(onehot, axis=0)
    padded = (counts + MOE_TM - 1) // MOE_TM * MOE_TM
    pend = jnp.cumsum(padded)
    dest = (pend - padded)[expert] + rank
    token_of_row = jnp.zeros((MOE_ROWS,), jnp.int32).at[dest].set(
        jnp.arange(T * TOP_K, dtype=jnp.int32) // TOP_K)
    blk_expert = jnp.minimum(
        jnp.searchsorted(pend, jnp.arange(MOE_BLOCKS, dtype=jnp.int32) * MOE_TM, side="right"),
        N_EXPERTS - 1).astype(jnp.int32)
    n_used = (pend[-1] // MOE_TM).astype(jnp.int32).reshape(1)
    return gates, dest.astype(jnp.int32).reshape(T, TOP_K), token_of_row, blk_expert, n_used


def _moe(h, logits, l, w_gate, b_gate, w_up, b_up, w_down, b_down):
    gates, dest, token_of_row, blk_expert, n_used = _route(logits)
    xs = _gather_rows(h, token_of_row, MOE_ROWS)
    y = _moe_experts(xs, blk_expert, n_used, l, w_gate, b_gate, w_up, b_up, w_down, b_down)
    return y, dest, gates


def kernel(x_prompt, x_sample, state_hgrn, cache_diff_k, cache_diff_v, cache_mla_ckv, cache_mla_kr, c, c_ctx,
           norm_mix, norm_ffn, w_ada, b_ada, hgrn_lb, w_in_even, g_hgrn_out, conv_w, conv_b, conv_ln_g, conv_ln_b,
           w_out_even, w_in_odd, w_uq, w_ukv, g_q, g_kv, diff_lambda, g_sub, w_out_odd,
           w_router, b_router, w_gate, b_gate, w_up, b_up, w_down, b_down, norm_final):
    x = jnp.concatenate([x_prompt.reshape(T_CTX, D), x_sample.reshape(T_LAT, D)], axis=0)
    cvec = jnp.concatenate([c_ctx[None, :], c, jnp.zeros((8 - 1 - DEC_BATCH, D), F32)], axis=0)
    ada = _ada(cvec, w_ada, b_ada)
    mods = [jnp.pad(ada[l, :1 + DEC_BATCH].reshape(1 + DEC_BATCH, 6, D), ((0, 0), (0, 2), (0, 0)))
            for l in range(DEPTH)]
    lower_bounds = jnp.cumsum(jax.nn.softmax(hgrn_lb.astype(F32), axis=0), axis=0)
    moe_w = lambda: (w_gate, b_gate, w_up, b_up, w_down, b_down)

    (h,) = _rmod(x, None, mods[0], norm_mix[0], mode="mod", mod_row=0)
    y0 = _mm(h, w_in_even[0], name="in_even")
    s0 = state_hgrn[:, 0].reshape(DEC_BATCH * 2 * A_HEADS, A_DK, A_DK)
    lb = lower_bounds[0].reshape(1, A_QK)
    o_fw, s_fw = _hgrn_dir(y0, lb, s0, rev=False)
    o_bw, s_bw = _hgrn_dir(y0, lb, s0, rev=True)
    o_a = _hgrn_out(o_fw, o_bw, y0, g_hgrn_out[0])
    o_b = _conv(y0, conv_w[0], conv_b[0], conv_ln_g[0], conv_ln_b[0])
    o = _mm(jnp.concatenate([o_a, o_b], axis=-1), w_out_even[0], name="out_even")
    x, h, logits = _rmod(x, o, mods[0], norm_ffn[0], gate_row=2, mode="mod", mod_row=3, h_dtype=F32,
                         router=(w_router, b_router, 0))
    y, dest, gates = _moe(h, logits, 0, *moe_w())
    x, h = _moe_combine(y, dest, gates, x, mods[0], norm_mix[1], mode="mod", mods_next=mods[1])
    new_hgrn = jnp.stack([s_fw, s_bw], axis=1)[:, None]

    w_in = jnp.concatenate([w_in_odd[0], w_in_odd[0][:, ODD_IN - ROPE:]], axis=1)
    y1 = _mm(h, w_in, name="in_odd")
    cq = y1[:, 3 * C_W:3 * C_W + Q_RANK]
    ckv_raw = y1[:, 3 * C_W + Q_RANK:3 * C_W + Q_RANK + KV_RANK]
    perm = np.concatenate([np.arange(D_HEADS)[:, None] * (NOPE + ROPE) + np.arange(NOPE)[None, :],
                           np.arange(D_HEADS)[:, None] * (NOPE + ROPE) + NOPE + np.arange(ROPE)[None, :]],
                          axis=None)
    qm = _mm(cq, w_uq[0][:, perm], g=g_q[0], tm=1024, tn=D_HEADS * (NOPE + ROPE), name="uq")
    kv, ckv = _mm(ckv_raw, w_ukv[0], g=g_kv[0], emit_norm=True, out_dtype=BF16, tm=1024, tn=2048, name="ukv")
    kv_c = _mm(cache_mla_ckv[:, 0].reshape(DEC_BATCH * PAST, KV_RANK), w_ukv[0], out_dtype=BF16,
               tm=1024, tn=2048, name="ukv_cache")
    kr_c = cache_mla_kr[:, 0].reshape(DEC_BATCH * PAST, ROPE)
    kr_c = jnp.concatenate([kr_c, kr_c], axis=-1)
    lq = diff_lambda[0].astype(F32)
    lam_init = 0.8 - 0.6 * math.exp(-0.3 * 1)
    lam = (jnp.exp(jnp.sum(lq[0] * lq[1])) - jnp.exp(jnp.sum(lq[2] * lq[3])) + lam_init).reshape(1, 1)
    ck = cache_diff_k[:, 0].reshape(DEC_BATCH * PAST, C_W)
    cv = cache_diff_v[:, 0].reshape(DEC_BATCH * PAST, C_W)
    o_c = jnp.concatenate([_diff_attn(y1, lam, g_sub[0], lam_init),
                           _diff_attn(y1, lam, g_sub[0], lam_init, ck, cv)], axis=0)
    o_d = jnp.concatenate([_mla_attn(qm, kv, y1), _mla_attn(qm, kv, y1, kv_c, kr_c)], axis=0)
    o = _mm(jnp.concatenate([o_c, o_d], axis=-1), w_out_odd[0], name="out_odd")
    x, h, logits = _rmod(x, o, mods[1], norm_ffn[1], gate_row=2, mode="mod", mod_row=3, h_dtype=F32,
                         router=(w_router, b_router, 1))
    y, dest, gates = _moe(h, logits, 1, *moe_w())
    (yout,) = _moe_combine(y, dest, gates, x, mods[1], norm_final, mode="final", h_dtype=F32)

    ctx = lambda a, shape: a[:T_CTX].reshape(shape)
    return (yout[:T_CTX].reshape(BATCH, SEQ, D), yout[T_CTX:].reshape(DEC_BATCH, DEC_SEQ, D),
            new_hgrn,
            ctx(y1[:, C_W:2 * C_W], (BATCH, 1, SEQ, C_HEADS, 2 * C_DH)),
            ctx(y1[:, 2 * C_W:3 * C_W], (BATCH, 1, SEQ, C_HEADS, 2 * C_DH)),
            ctx(ckv, (BATCH, 1, SEQ, KV_RANK)),
            ctx(y1[:, ODD_IN - ROPE:ODD_IN], (BATCH, 1, SEQ, ROPE)))
```

```python
import functools
import math

import numpy as np
import jax
import jax.numpy as jnp
from jax import lax
from jax.experimental import pallas as pl
from jax.experimental.pallas import tpu as pltpu

F32 = jnp.float32
BF16 = jnp.bfloat16

D = 2048
BATCH, SEQ = 32, 256
DEC_BATCH, DEC_SEQ = 2, 1024
PAST = 512
DEPTH = 2
GRID_W = 64
T_CTX = BATCH * SEQ
T_LAT = DEC_BATCH * DEC_SEQ
T = T_CTX + T_LAT
A_HEADS, A_DK = 8, 128
A_QK = A_HEADS * A_DK
B_W = 1024
CONV_K = 31
C_HEADS, C_DH = 4, 128
C_W = C_HEADS * 2 * C_DH
D_HEADS = 8
Q_RANK, KV_RANK = 512, 256
NOPE, ROPE, V_DIM = 128, 64, 128
ROPE_BASE = 10000.0
N_EXPERTS, TOP_K = 32, 4
D_FF = 2048
SWIGLU_ALPHA, SWIGLU_LIMIT = 1.702, 7.0
EVEN_IN = 3 * A_QK + 2 * A_QK + 2 * B_W
ODD_IN = 3 * C_W + Q_RANK + KV_RANK + ROPE
EPS = 1e-6

LANE = 128
UNIT = 256
ROW_TILE = 256
MM_TM, MM_TN = 512, 1024
MOE_TM = 256
MOE_TN_UP = 1024
MOE_TN_DOWN = 2048
MOE_ROWS = T * TOP_K + N_EXPERTS * MOE_TM
MOE_BLOCKS = MOE_ROWS // MOE_TM
ROUTE_TB = 256
DISPATCH_TB = 256
COMBINE_TB = 128
VMEM_LIMIT = 56 * 1024 * 1024
HGRN_LEVELS = (16, 32, 64, 128, 256)
HGRN_CLAMP = 40.0


def _cparams(*sem):
    return pltpu.CompilerParams(dimension_semantics=sem, vmem_limit_bytes=VMEM_LIMIT)


def _dot(a, b):
    return lax.dot_general(a, b, (((1,), (0,)), ((), ())), preferred_element_type=F32)


def _dot_nt(a, b):
    return lax.dot_general(a, b, (((1,), (1,)), ((), ())), preferred_element_type=F32)


def _dot_tn(a, b):
    return lax.dot_general(a, b, (((0,), (0,)), ((), ())), preferred_element_type=F32)


def _sigmoid(x):
    return 1.0 / (1.0 + jnp.exp(-x))


def _silu(x):
    return x * _sigmoid(x)


def _rms(x, g):
    return x * lax.rsqrt(jnp.mean(x * x, axis=-1, keepdims=True) + EPS) * g


def _seg_of_block(i, rows):
    nctx = T_CTX // rows
    return jnp.where(i < nctx, 0, 1 + (i - nctx) // (DEC_SEQ // rows))


def _ada_kernel(c_ref, w_ref, b_ref, o_ref):
    a = _silu(c_ref[...]).astype(BF16)
    o_ref[...] = _dot(a, w_ref[...].astype(BF16)) + b_ref[...]


def _ada(cvec, w_ada, b_ada):
    tn = 1024
    return pl.pallas_call(
        _ada_kernel,
        grid=(DEPTH, 6 * D // tn),
        in_specs=[pl.BlockSpec((8, D), lambda l, j: (0, 0)),
                  pl.BlockSpec((None, D, tn), lambda l, j: (l, 0, j)),
                  pl.BlockSpec((None, 1, tn), lambda l, j: (l, 0, j))],
        out_specs=pl.BlockSpec((None, 8, tn), lambda l, j: (l, 0, j)),
        out_shape=jax.ShapeDtypeStruct((DEPTH, 8, 6 * D), F32),
        compiler_params=_cparams("arbitrary", "arbitrary"),
        name="ada",
    )(cvec, w_ada, b_ada.reshape(DEPTH, 1, 6 * D))


def _mm_kernel(*refs, rms, emit_norm):
    it = iter(refs)
    x_ref, w_ref = next(it), next(it)
    g_ref = next(it) if rms else None
    o_ref = next(it)
    n_ref = next(it) if emit_norm else None
    wbf = next(it)

    @pl.when(pl.program_id(1) == 0)
    def _():
        wbf[...] = w_ref[...].astype(BF16)

    x = x_ref[...]
    if rms:
        x = _rms(x.astype(F32), g_ref[...])
        if emit_norm:
            n_ref[...] = x
    o_ref[...] = _dot(x.astype(BF16), wbf[...]).astype(o_ref.dtype)


def _mm(x, w, *, out_dtype=F32, tm=MM_TM, tn=MM_TN, g=None, emit_norm=False, name="mm"):
    M, K = x.shape
    N = w.shape[1]
    tn = min(tn, N)
    nj = pl.cdiv(N, tn)
    assert M % tm == 0 and (not emit_norm or nj == 1)
    in_specs = [pl.BlockSpec((tm, K), lambda j, i: (i, 0)),
                pl.BlockSpec((K, tn), lambda j, i: (0, j))]
    args = [x, w]
    if g is not None:
        in_specs.append(pl.BlockSpec((1, K), lambda j, i: (0, 0)))
        args.append(g.reshape(1, K))
    out_specs = [pl.BlockSpec((tm, tn), lambda j, i: (i, j))]
    out_shape = [jax.ShapeDtypeStruct((M, N), out_dtype)]
    if emit_norm:
        out_specs.append(pl.BlockSpec((tm, K), lambda j, i: (i, 0)))
        out_shape.append(jax.ShapeDtypeStruct((M, K), F32))
    res = pl.pallas_call(
        functools.partial(_mm_kernel, rms=g is not None, emit_norm=emit_norm),
        grid=(nj, M // tm),
        in_specs=in_specs, out_specs=out_specs, out_shape=out_shape,
        scratch_shapes=[pltpu.VMEM((K, tn), BF16)],
        compiler_params=_cparams("arbitrary", "arbitrary"),
        name=name,
    )(*args)
    return res if emit_norm else res[0]


def _resid_norm(x, o, mods, g, *, has_resid, gate_row, mode, mod_row):
    if has_resid:
        x = x + mods[gate_row:gate_row + 1, :] * o
    h = _rms(x, g)
    if mode == "mod":
        h = h * (1.0 + mods[mod_row + 1:mod_row + 2, :]) + mods[mod_row:mod_row + 1, :]
    return x, h


def _rmod_kernel(*refs, has_resid, gate_row, mode, mod_row, emit_logits, rows3d):
    it = iter(refs)
    x_ref = next(it)
    o_ref = next(it) if has_resid else None
    mods_ref, g_ref = next(it), next(it)
    if emit_logits:
        wr_ref, br_ref = next(it), next(it)
    x1_ref = next(it) if has_resid else None
    h_ref = next(it)
    x, h = _resid_norm(x_ref[...], o_ref[...] if has_resid else None, mods_ref[...], g_ref[...],
                       has_resid=has_resid, gate_row=gate_row, mode=mode, mod_row=mod_row)
    if has_resid:
        x1_ref[...] = x
    if rows3d:
        for cc in range(D // LANE):
            h_ref[:, cc, :] = h[:, cc * LANE:(cc + 1) * LANE]
    else:
        h_ref[...] = h.astype(h_ref.dtype)
    if emit_logits:
        lg_ref = next(it)
        lg_ref[...] = lax.dot_general(h, wr_ref[...], (((1,), (0,)), ((), ())),
                                      precision=lax.Precision.HIGHEST,
                                      preferred_element_type=F32) + br_ref[...]


def _rmod(x, o, mods, g, *, gate_row=0, mode="mod", mod_row=0, h_dtype=BF16, router=None, rows3d=False):
    tm = ROW_TILE
    has_resid = o is not None
    row = pl.BlockSpec((tm, D), lambda i: (i, 0))
    in_specs, args = [row], [x]
    if has_resid:
        in_specs.append(row)
        args.append(o)
    in_specs += [pl.BlockSpec((None, 8, D), lambda i: (_seg_of_block(i, tm), 0, 0)),
                 pl.BlockSpec((1, D), lambda i: (0, 0))]
    args += [mods, g.reshape(1, D)]
    if router is not None:
        w_router, b_router, l = router
        in_specs += [pl.BlockSpec((None, D, N_EXPERTS), lambda i: (l, 0, 0)),
                     pl.BlockSpec((None, 1, N_EXPERTS), lambda i: (l, 0, 0))]
        args += [w_router, b_router.reshape(DEPTH, 1, N_EXPERTS)]
    out_specs, out_shape = [], []
    if has_resid:
        out_specs.append(row)
        out_shape.append(jax.ShapeDtypeStruct((T, D), F32))
    if rows3d:
        out_specs.append(pl.BlockSpec((tm, D // LANE, LANE), lambda i: (i, 0, 0)))
        out_shape.append(jax.ShapeDtypeStruct((T, D // LANE, LANE), F32))
    else:
        out_specs.append(row)
        out_shape.append(jax.ShapeDtypeStruct((T, D), h_dtype))
    if router is not None:
        out_specs.append(pl.BlockSpec((tm, N_EXPERTS), lambda i: (i, 0)))
        out_shape.append(jax.ShapeDtypeStruct((T, N_EXPERTS), F32))
    return pl.pallas_call(
        functools.partial(_rmod_kernel, has_resid=has_resid, gate_row=gate_row, mode=mode,
                          mod_row=mod_row, emit_logits=router is not None, rows3d=rows3d),
        grid=(T // tm,), in_specs=in_specs, out_specs=out_specs, out_shape=out_shape,
        compiler_params=_cparams("arbitrary"),
        name="rmod",
    )(*args)


def _hgrn_consts(rev):
    L = UNIT
    t = np.arange(L)[:, None]
    s = np.arange(L)[None, :]
    order = (s >= t) if rev else (s <= t)
    masks = [((t // HGRN_LEVELS[0]) == (s // HGRN_LEVELS[0])) & order]
    masks += [(t // b) == (s // b) for b in HGRN_LEVELS[1:]]
    return jnp.asarray(order, BF16), jnp.asarray(np.stack(masks), F32)


def _hgrn_kernel(q_ref, v_ref, f_ref, lb_ref, s0_ref, tri_ref, msk_ref, o_ref, so_ref, st, *, rev):
    L = UNIT
    n = pl.program_id(1)
    is_ctx = n < BATCH

    @pl.when(is_ctx)
    def _():
        st[...] = jnp.zeros_like(st)

    @pl.when(jnp.logical_and(n >= BATCH, (n - BATCH) % (DEC_SEQ // L) == 0))
    def _():
        st[...] = s0_ref[...].T

    q = _silu(q_ref[...])
    v = v_ref[...].astype(BF16)
    lb = lb_ref[...]
    f = lb + (1.0 - lb) * _sigmoid(f_ref[...])
    k = 1.0 - f
    logf = jnp.log(f)
    hi = logf.astype(BF16)
    r1 = logf - hi.astype(F32)
    mid = r1.astype(BF16)
    lo = (r1 - mid.astype(F32)).astype(BF16)
    tri = tri_ref[...]
    bcum = _dot(tri, hi) + _dot(tri, mid) + _dot(tri, lo)

    rowi = lax.broadcasted_iota(jnp.int32, (L, A_DK), 0)
    scores = jnp.zeros((L, L), F32)
    for li, b in enumerate(HGRN_LEVELS):
        r = b // 2 if (rev or li == 0) else b // 2 - 1
        b3 = bcum.reshape(L // b, b, A_DK)
        ref = jnp.broadcast_to(b3[:, r:r + 1, :], (L // b, b, A_DK)).reshape(L, A_DK)
        dlt = bcum - ref
        if li == 0:
            qt = q * jnp.exp(jnp.clip(dlt, -HGRN_CLAMP, HGRN_CLAMP))
            kt = k * jnp.exp(jnp.clip(-dlt, -HGRN_CLAMP, HGRN_CLAMP))
        else:
            later = (rowi % b) >= (b // 2)
            q_rows = jnp.logical_not(later) if rev else later
            e = jnp.exp(-jnp.abs(dlt))
            qt = jnp.where(q_rows, q * e, 0.0)
            kt = jnp.where(q_rows, 0.0, k * e)
        scores = scores + msk_ref[li] * _dot_nt(qt.astype(BF16), kt.astype(BF16))

    s_prev = st[...]
    o = _dot(scores.astype(BF16), v) + _dot_nt((q * jnp.exp(bcum)).astype(BF16), s_prev.astype(BF16))
    o_ref[...] = o
    b_end = bcum[0:1, :] if rev else bcum[L - 1:L, :]
    kk = (k * jnp.exp(b_end - bcum)).astype(BF16)
    s_new = s_prev * jnp.exp(b_end) + _dot_tn(v, kk)
    st[...] = s_new

    @pl.when(is_ctx)
    def _():
        so_ref[...] = s_new.T


def _hgrn_dir(y0, lb, s0, *, rev):
    L = UNIT
    per = DEC_SEQ // L
    nb = A_QK // LANE
    d = 1 if rev else 0
    tri, msk = _hgrn_consts(rev)

    def rb(n):
        if not rev:
            return n
        m = n - BATCH
        return jnp.where(n < BATCH, n, BATCH + (m // per) * per + (per - 1 - m % per))

    def s0_idx(h, n):
        b = jnp.clip((n - BATCH) // per, 0, DEC_BATCH - 1)
        return (b * 2 * A_HEADS + d * A_HEADS + h, 0, 0)

    o, so = pl.pallas_call(
        functools.partial(_hgrn_kernel, rev=rev),
        grid=(A_HEADS, T // L),
        in_specs=[pl.BlockSpec((L, LANE), lambda h, n: (rb(n), h)),
                  pl.BlockSpec((L, LANE), lambda h, n: (rb(n), nb + h)),
                  pl.BlockSpec((L, LANE), lambda h, n: (rb(n), (2 + d) * nb + h)),
                  pl.BlockSpec((1, LANE), lambda h, n: (0, h)),
                  pl.BlockSpec((None, A_DK, A_DK), s0_idx),
                  pl.BlockSpec((L, L), lambda h, n: (0, 0)),
                  pl.BlockSpec((len(HGRN_LEVELS), L, L), lambda h, n: (0, 0, 0))],
        out_specs=[pl.BlockSpec((L, LANE), lambda h, n: (rb(n), h)),
                   pl.BlockSpec((None, A_DK, A_DK),
                                lambda h, n: (jnp.minimum(n, BATCH - 1) * A_HEADS + h, 0, 0))],
        out_shape=[jax.ShapeDtypeStruct((T, A_QK), F32),
                   jax.ShapeDtypeStruct((BATCH * A_HEADS, A_DK, A_DK), F32)],
        scratch_shapes=[pltpu.VMEM((A_DK, A_DK), F32)],
        compiler_params=_cparams("arbitrary", "arbitrary"),
        name="hgrn_bwd" if rev else "hgrn_fwd",
    )(y0, y0, y0, lb, s0, tri, msk)
    return o, so.reshape(BATCH, A_HEADS, A_DK, A_DK)


def _hgrn_out_kernel(of_ref, ob_ref, g_ref, gn_ref, o_ref):
    o = _rms(of_ref[...] + ob_ref[...], gn_ref[...])
    o_ref[...] = (o * _silu(g_ref[...])).astype(o_ref.dtype)


def _hgrn_out(o_fw, o_bw, y0, g_out):
    tm = 1024
    nb = A_QK // LANE
    blk = lambda off: pl.BlockSpec((tm, LANE), lambda i, h: (i, off + h))
    return pl.pallas_call(
        _hgrn_out_kernel,
        grid=(T // tm, A_HEADS),
        in_specs=[blk(0), blk(0), blk(4 * nb), pl.BlockSpec((1, LANE), lambda i, h: (0, 0))],
        out_specs=blk(0),
        out_shape=jax.ShapeDtypeStruct((T, A_QK), BF16),
        compiler_params=_cparams("arbitrary", "arbitrary"),
        name="hgrn_out",
    )(o_fw, o_bw, y0, g_out.reshape(1, LANE))


CONV_HALO = 16
CONV_RC, CONV_CC = 32, 256


def _conv_kernel(ap_ref, gp_ref, a_ref, g_ref, an_ref, gn_ref, w_ref, b_ref, lg_ref, lb_ref,
                 o_ref, pad, yb):
    L = UNIT
    n = pl.program_id(0)
    s = (n - BATCH) % (DEC_SEQ // L)
    lat = n >= BATCH
    has_prev = jnp.logical_and(lat, s > 0).astype(F32)
    has_next = jnp.logical_and(lat, s < DEC_SEQ // L - 1).astype(F32)
    glu = lambda a, g: a[...] * _sigmoid(g[...])
    pad[0:CONV_HALO, :] = glu(ap_ref, gp_ref) * has_prev
    pad[CONV_HALO:CONV_HALO + L, :] = glu(a_ref, g_ref)
    pad[CONV_HALO + L:CONV_HALO + L + CONV_HALO, :] = glu(an_ref, gn_ref) * has_next
    base = CONV_HALO - CONV_K // 2
    for c0 in range(0, B_W, CONV_CC):
        w = w_ref[:, c0:c0 + CONV_CC]
        for r0 in range(0, L, CONV_RC):
            acc = jnp.zeros((CONV_RC, CONV_CC), F32)
            for kk in range(CONV_K):
                acc = acc + w[kk:kk + 1, :] * pad[r0 + base + kk:r0 + base + kk + CONV_RC, c0:c0 + CONV_CC]
            yb[r0:r0 + CONV_RC, c0:c0 + CONV_CC] = acc
    y = yb[...] + b_ref[...]
    mu = jnp.mean(y, axis=-1, keepdims=True)
    yc = y - mu
    yn = yc * lax.rsqrt(jnp.mean(yc * yc, axis=-1, keepdims=True) + EPS) * lg_ref[...] + lb_ref[...]
    o_ref[...] = _silu(yn).astype(o_ref.dtype)


def _conv(y0, conv_w, conv_b, ln_g, ln_b):
    L = UNIT
    ca = (3 * A_QK + 2 * A_QK) // B_W
    hb = L // CONV_HALO
    nhalo = T // CONV_HALO
    prev = lambda off: pl.BlockSpec((CONV_HALO, B_W), lambda n: (jnp.maximum(n * hb - 1, 0), off))
    cur = lambda off: pl.BlockSpec((L, B_W), lambda n: (n, off))
    nxt = lambda off: pl.BlockSpec((CONV_HALO, B_W), lambda n: (jnp.minimum((n + 1) * hb, nhalo - 1), off))
    vec = pl.BlockSpec((1, B_W), lambda n: (0, 0))
    return pl.pallas_call(
        _conv_kernel,
        grid=(T // L,),
        in_specs=[prev(ca), prev(ca + 1), cur(ca), cur(ca + 1), nxt(ca), nxt(ca + 1),
                  pl.BlockSpec((CONV_K, B_W), lambda n: (0, 0)), vec, vec, vec],
        out_specs=pl.BlockSpec((L, B_W), lambda n: (n, 0)),
        out_shape=jax.ShapeDtypeStruct((T, B_W), BF16),
        scratch_shapes=[pltpu.VMEM((L + 2 * CONV_HALO, B_W), F32), pltpu.VMEM((L, B_W), F32)],
        compiler_params=_cparams("arbitrary"),
        name="conv",
    )(y0, y0, y0, y0, y0, y0, conv_w, conv_b.reshape(1, B_W), ln_g.reshape(1, B_W), ln_b.reshape(1, B_W))


def _axial_tables(L, rot_dim):
    rows = L // GRID_W
    row = np.repeat(np.arange(rows), GRID_W).astype(np.float32)
    col = np.tile(np.arange(GRID_W), rows).astype(np.float32)
    quarter = rot_dim // 4
    inv = (ROPE_BASE ** (-np.arange(quarter, dtype=np.float32) / quarter)).astype(np.float32)
    ang = np.concatenate([row[:, None] * inv, col[:, None] * inv], axis=-1)
    return np.cos(ang).astype(np.float32), np.sin(ang).astype(np.float32)


def _rope_tables(L, rot_dim, reps):
    cos, sin = _axial_tables(L, rot_dim)
    c = np.tile(np.concatenate([cos, cos], axis=-1), (1, reps))
    s = np.tile(np.concatenate([-sin, sin], axis=-1), (1, reps))
    return jnp.asarray(c), jnp.asarray(s)


def _rope(x, c, s, rot_dim):
    w = x.shape[-1]
    half = rot_dim // 2
    if rot_dim == w:
        swapped = pltpu.roll(x, half, 1)
    else:
        lane = lax.broadcasted_iota(jnp.int32, x.shape, 1)
        swapped = jnp.where((lane % rot_dim) < half, pltpu.roll(x, w - half, 1), pltpu.roll(x, half, 1))
    return x * c + swapped * s


def _softmax_parts(scores):
    m = scores[0].max(axis=-1, keepdims=True)
    for s in scores[1:]:
        m = jnp.maximum(m, s.max(axis=-1, keepdims=True))
    es = [jnp.exp(s - m) for s in scores]
    den = es[0].sum(axis=-1, keepdims=True)
    for e in es[1:]:
        den = den + e.sum(axis=-1, keepdims=True)
    inv = 1.0 / den
    return [e * inv for e in es]


def _diff_kernel(*refs, two_seg, lam_init):
    it = iter(refs)
    lam_ref, q1_ref, q2_ref = next(it), next(it), next(it)
    if two_seg:
        k1c_ref, k2c_ref, vc_ref = next(it), next(it), next(it)
    k1_ref, k2_ref, v_ref = next(it), next(it), next(it)
    if two_seg:
        cq_ref, sq_ref, ck_ref, sk_ref = next(it), next(it), next(it), next(it)
    gs_ref, o_ref = next(it), next(it)
    scale = C_DH ** -0.5
    lam = lam_ref[0, 0]
    outs = []
    ps = []
    for q_ref, kc_ref, k_ref in ((q1_ref, k1c_ref if two_seg else None, k1_ref),
                                 (q2_ref, k2c_ref if two_seg else None, k2_ref)):
        q = q_ref[...]
        k = k_ref[...]
        if two_seg:
            sc = [_dot_nt(q.astype(BF16), kc_ref[...].astype(BF16)) * scale,
                  _dot_nt(_rope(q, cq_ref[...], sq_ref[...], C_DH).astype(BF16),
                          _rope(k, ck_ref[...], sk_ref[...], C_DH).astype(BF16)) * scale]
        else:
            sc = [_dot_nt(q.astype(BF16), k.astype(BF16)) * scale]
        ps.append(_softmax_parts(sc))
    vs = ([vc_ref[...]] if two_seg else []) + [v_ref[...]]
    o = None
    for p1, p2, vv in zip(ps[0], ps[1], vs):
        t = _dot((p1 - lam * p2).astype(BF16), vv.astype(BF16))
        o = t if o is None else o + t
    o_ref[...] = (_rms(o, gs_ref[...]) * (1.0 - lam_init)).astype(o_ref.dtype)


def _diff_attn(y1, lam, g_sub, lam_init, cache_k=None, cache_v=None):
    two_seg = cache_k is not None
    hw = 2 * C_DH
    smem = pl.BlockSpec(memory_space=pltpu.SMEM)
    if not two_seg:
        grid = (BATCH, C_HEADS)
        blk = lambda off, w: pl.BlockSpec((SEQ, w), lambda b, h: (b, off + h * (hw // w)))
        in_specs = [smem, blk(0, C_DH), blk(1, C_DH),
                    blk(C_W // C_DH, C_DH), blk(C_W // C_DH + 1, C_DH), blk(2 * C_W // hw, hw),
                    pl.BlockSpec((1, hw), lambda b, h: (0, 0))]
        args = [lam, y1, y1, y1, y1, y1, g_sub.reshape(1, hw)]
        out_specs = pl.BlockSpec((SEQ, hw), lambda b, h: (b, h))
        out_rows = T_CTX
        sem = ("arbitrary", "arbitrary")
    else:
        tq = 256
        nq = DEC_SEQ // tq
        r0 = T_CTX // DEC_SEQ
        grid = (DEC_BATCH, C_HEADS, nq)
        qblk = lambda off: pl.BlockSpec((tq, C_DH), lambda b, h, i: (T_CTX // tq + b * nq + i, off + 2 * h))
        kblk = lambda off, w: pl.BlockSpec((DEC_SEQ, w), lambda b, h, i: (r0 + b, off + h * (hw // w)))
        cblk = lambda off, w: pl.BlockSpec((PAST, w), lambda b, h, i: (b, off + h * (hw // w)))
        tq_blk = pl.BlockSpec((tq, C_DH), lambda b, h, i: (i, 0))
        tk_blk = pl.BlockSpec((DEC_SEQ, C_DH), lambda b, h, i: (0, 0))
        c, s = _rope_tables(DEC_SEQ, C_DH, 1)
        in_specs = [smem, qblk(0), qblk(1), cblk(0, C_DH), cblk(1, C_DH), cblk(0, hw),
                    kblk(C_W // C_DH, C_DH), kblk(C_W // C_DH + 1, C_DH), kblk(2 * C_W // hw, hw),
                    tq_blk, tq_blk, tk_blk, tk_blk,
                    pl.BlockSpec((1, hw), lambda b, h, i: (0, 0))]
        args = [lam, y1, y1, cache_k, cache_k, cache_v, y1, y1, y1, c, s, c, s, g_sub.reshape(1, hw)]
        out_specs = pl.BlockSpec((tq, hw), lambda b, h, i: (b * nq + i, h))
        out_rows = T_LAT
        sem = ("arbitrary", "arbitrary", "arbitrary")
    return pl.pallas_call(
        functools.partial(_diff_kernel, two_seg=two_seg, lam_init=lam_init),
        grid=grid, in_specs=in_specs, out_specs=out_specs,
        out_shape=jax.ShapeDtypeStruct((out_rows, C_W), BF16),
        compiler_params=_cparams(*sem),
        name="diff_lat" if two_seg else "diff_ctx",
    )(*args)


def _mla_kernel(*refs, two_seg):
    it = iter(refs)
    qn_ref, qr_ref = next(it), next(it)
    if two_seg:
        kvc_ref, krc_ref = next(it), next(it)
    kv_ref, kr_ref = next(it), next(it)
    if two_seg:
        cq_ref, sq_ref, ck_ref, sk_ref = next(it), next(it), next(it), next(it)
    o_ref = next(it)
    scale = (NOPE + ROPE) ** -0.5
    qr = qr_ref[...]
    kr = kr_ref[...]
    if two_seg:
        qr_rot = _rope(qr, cq_ref[...], sq_ref[...], ROPE)
        kr_rot = _rope(kr, ck_ref[...], sk_ref[...], ROPE)
        krc = krc_ref[...]
    lane = lax.broadcasted_iota(jnp.int32, (1, LANE), 1)
    for h in range(D_HEADS):
        half = (lane < ROPE) if h % 2 == 0 else (lane >= ROPE)
        pr = slice((h // 2) * LANE, (h // 2 + 1) * LANE)
        qn = qn_ref[:, h * NOPE:(h + 1) * NOPE]
        kn = kv_ref[:, h * 2 * NOPE:h * 2 * NOPE + NOPE]
        vv = kv_ref[:, h * 2 * NOPE + NOPE:(h + 1) * 2 * NOPE]
        cat = lambda a, b: jnp.concatenate([a.astype(BF16), b.astype(BF16)], axis=-1)
        if two_seg:
            knc = kvc_ref[:, h * 2 * NOPE:h * 2 * NOPE + NOPE]
            vc = kvc_ref[:, h * 2 * NOPE + NOPE:(h + 1) * 2 * NOPE]
            sc = [_dot_nt(cat(qn, qr[:, pr]), cat(knc, jnp.where(half, krc, 0.0))) * scale,
                  _dot_nt(cat(qn, qr_rot[:, pr]), cat(kn, jnp.where(half, kr_rot, 0.0))) * scale]
            p = _softmax_parts(sc)
            o = _dot(p[0].astype(BF16), vc) + _dot(p[1].astype(BF16), vv)
        else:
            sc = [_dot_nt(cat(qn, qr[:, pr]), cat(kn, jnp.where(half, kr, 0.0))) * scale]
            o = _dot(_softmax_parts(sc)[0].astype(BF16), vv)
        o_ref[:, h * V_DIM:(h + 1) * V_DIM] = o.astype(o_ref.dtype)


def _mla_attn(qm, kv, y1, kv_c=None, kr_c=None):
    two_seg = kv_c is not None
    qn_w, qr_w = D_HEADS * NOPE, D_HEADS * ROPE
    kr_col = (ODD_IN - ROPE) // LANE
    if not two_seg:
        grid = (BATCH,)
        in_specs = [pl.BlockSpec((SEQ, qn_w), lambda b: (b, 0)),
                    pl.BlockSpec((SEQ, qr_w), lambda b: (b, qn_w // qr_w)),
                    pl.BlockSpec((SEQ, 2 * qn_w), lambda b: (b, 0)),
                    pl.BlockSpec((SEQ, LANE), lambda b: (b, kr_col))]
        args = [qm, qm, kv, y1]
        out_specs = pl.BlockSpec((SEQ, qn_w), lambda b: (b, 0))
        out_rows = T_CTX
        sem = ("arbitrary",)
    else:
        tq = 256
        nq = DEC_SEQ // tq
        r0 = T_CTX // DEC_SEQ
        grid = (DEC_BATCH, nq)
        cq, sq = _rope_tables(DEC_SEQ, ROPE, qr_w // ROPE)
        ck, sk = _rope_tables(DEC_SEQ, ROPE, LANE // ROPE)
        in_specs = [pl.BlockSpec((tq, qn_w), lambda b, i: (T_CTX // tq + b * nq + i, 0)),
                    pl.BlockSpec((tq, qr_w), lambda b, i: (T_CTX // tq + b * nq + i, qn_w // qr_w)),
                    pl.BlockSpec((PAST, 2 * qn_w), lambda b, i: (b, 0)),
                    pl.BlockSpec((PAST, LANE), lambda b, i: (b, 0)),
                    pl.BlockSpec((DEC_SEQ, 2 * qn_w), lambda b, i: (r0 + b, 0)),
                    pl.BlockSpec((DEC_SEQ, LANE), lambda b, i: (r0 + b, kr_col)),
                    pl.BlockSpec((tq, qr_w), lambda b, i: (i, 0)),
                    pl.BlockSpec((tq, qr_w), lambda b, i: (i, 0)),
                    pl.BlockSpec((DEC_SEQ, LANE), lambda b, i: (0, 0)),
                    pl.BlockSpec((DEC_SEQ, LANE), lambda b, i: (0, 0))]
        args = [qm, qm, kv_c, kr_c, kv, y1, cq, sq, ck, sk]
        out_specs = pl.BlockSpec((tq, qn_w), lambda b, i: (b * nq + i, 0))
        out_rows = T_LAT
        sem = ("arbitrary", "arbitrary")
    return pl.pallas_call(
        functools.partial(_mla_kernel, two_seg=two_seg),
        grid=grid, in_specs=in_specs, out_specs=out_specs,
        out_shape=jax.ShapeDtypeStruct((out_rows, qn_w), BF16),
        compiler_params=_cparams(*sem),
        name="mla_lat" if two_seg else "mla_ctx",
    )(*args)


def _route_kernel(lg_ref, tril_ref, o_ref, cnt_ref, carry):
    tb = ROUTE_TB

    @pl.when(pl.program_id(0) == 0)
    def _():
        carry[...] = jnp.zeros_like(carry)

    l = lg_ref[...]
    lane = lax.broadcasted_iota(jnp.int32, (tb, N_EXPERTS), 1).astype(F32)
    vals, ids, sels = [], [], []
    for _ in range(TOP_K):
        m = l.max(axis=-1, keepdims=True)
        idx = jnp.min(jnp.where(l == m, lane, float(N_EXPERTS)), axis=-1, keepdims=True)
        sel = lane == idx
        vals.append(m)
        ids.append(idx)
        sels.append(sel)
        l = jnp.where(sel, -jnp.inf, l)
    es = [jnp.exp(v - vals[0]) for v in vals]
    den = es[0]
    for e in es[1:]:
        den = den + e
    inv = 1.0 / den
    picked = jnp.zeros((tb, N_EXPERTS), F32)
    for sel in sels:
        picked = picked + jnp.where(sel, 1.0, 0.0)
    base = carry[...] + _dot(tril_ref[...], picked.astype(BF16))
    carry[...] = carry[...] + jnp.sum(picked, axis=0, keepdims=True)
    cnt_ref[...] = carry[...]
    out_lane = lax.broadcasted_iota(jnp.int32, (tb, LANE), 1)
    out = jnp.zeros((tb, LANE), F32)
    for k in range(TOP_K):
        rank = jnp.sum(jnp.where(sels[k], base, 0.0), axis=-1, keepdims=True)
        out = jnp.where(out_lane == k, ids[k], out)
        out = jnp.where(out_lane == TOP_K + k, es[k] * inv, out)
        out = jnp.where(out_lane == 2 * TOP_K + k, rank, out)
    o_ref[...] = out


def _route(logits):
    tb = ROUTE_TB
    tril = jnp.asarray(np.tril(np.ones((tb, tb), np.float32), -1), BF16)
    packed, counts = pl.pallas_call(
        _route_kernel,
        grid=(T // tb,),
        in_specs=[pl.BlockSpec((tb, N_EXPERTS), lambda i: (i, 0)),
                  pl.BlockSpec((tb, tb), lambda i: (0, 0))],
        out_specs=[pl.BlockSpec((tb, LANE), lambda i: (i, 0)),
                   pl.BlockSpec((1, N_EXPERTS), lambda i: (0, 0))],
        out_shape=[jax.ShapeDtypeStruct((T, LANE), F32), jax.ShapeDtypeStruct((1, N_EXPERTS), F32)],
        scratch_shapes=[pltpu.VMEM((1, N_EXPERTS), F32)],
        compiler_params=_cparams("arbitrary"),
        name="moe_route",
    )(logits, tril)
    eid = packed[:, 0:TOP_K].astype(jnp.int32)
    gates = packed[:, TOP_K:2 * TOP_K]
    rank = packed[:, 2 * TOP_K:3 * TOP_K].astype(jnp.int32)
    return eid, gates, rank, counts[0].astype(jnp.int32)


def _smem_rows(a, tb):
    return a.reshape(T // tb, tb, TOP_K).transpose(0, 2, 1).reshape(T // tb, 1, TOP_K * tb)


def _dispatch_kernel(pstart_ref, padst_ref, padn_ref, nu_ref, eid_ref, rank_ref, h_ref, xs_ref,
                     zblk, sem, zsem):
    tb = DISPATCH_TB
    i = pl.program_id(0)

    def body(t, c):
        for k in range(TOP_K):
            d = pstart_ref[eid_ref[0, 0, k * tb + t]] + rank_ref[0, 0, k * tb + t]
            pltpu.make_async_copy(h_ref.at[t], xs_ref.at[d], sem).start()
        return c
    lax.fori_loop(0, tb, body, 0)

    def pad_rows(fn):
        def per_e(e, c):
            return lax.fori_loop(0, padn_ref[e], lambda r, c2: fn(padst_ref[e] + r, c2), c)
        lax.fori_loop(0, N_EXPERTS, per_e, 0)

    def pad_blocks(fn):
        lax.fori_loop(nu_ref[0], MOE_BLOCKS, fn, 0)

    row_copy = lambda d: pltpu.make_async_copy(zblk.at[0], xs_ref.at[d], zsem)
    blk_copy = lambda b: pltpu.make_async_copy(
        zblk, xs_ref.at[pl.ds(pl.multiple_of(b * MOE_TM, MOE_TM), MOE_TM)], zsem)

    @pl.when(i == 0)
    def _():
        zblk[...] = jnp.zeros_like(zblk)

        def start_row(d, c):
            row_copy(d).start()
            return c

        def start_blk(b, c):
            blk_copy(b).start()
            return c

        def wait_row(d, c):
            row_copy(d).wait()
            return c

        def wait_blk(b, c):
            blk_copy(b).wait()
            return c
        pad_rows(start_row)
        pad_blocks(start_blk)
        pad_rows(wait_row)
        pad_blocks(wait_blk)

    for k in range(TOP_K):
        pltpu.make_async_copy(h_ref, xs_ref.at[pl.ds(0, tb)], sem).wait()


def _dispatch(h3, eid, rank, counts):
    tb = DISPATCH_TB
    padded = (counts + MOE_TM - 1) // MOE_TM * MOE_TM
    pend = jnp.cumsum(padded)
    pstart = pend - padded
    n_used = (pend[-1:] // MOE_TM).astype(jnp.int32)
    blk_expert = jnp.minimum(
        jnp.searchsorted(pend, jnp.arange(MOE_BLOCKS, dtype=jnp.int32) * MOE_TM, side="right"),
        N_EXPERTS - 1).astype(jnp.int32)
    smem = lambda: pl.BlockSpec((1, 1, TOP_K * tb), lambda i, *_: (i, 0, 0), memory_space=pltpu.SMEM)
    xs = pl.pallas_call(
        _dispatch_kernel,
        grid_spec=pltpu.PrefetchScalarGridSpec(
            num_scalar_prefetch=4, grid=(T // tb,),
            in_specs=[smem(), smem(), pl.BlockSpec((tb, D // LANE, LANE), lambda i, *_: (i, 0, 0))],
            out_specs=pl.BlockSpec(memory_space=pl.ANY),
            scratch_shapes=[pltpu.VMEM((MOE_TM, D // LANE, LANE), F32),
                            pltpu.SemaphoreType.DMA, pltpu.SemaphoreType.DMA]),
        out_shape=jax.ShapeDtypeStruct((MOE_ROWS, D // LANE, LANE), F32),
        compiler_params=_cparams("arbitrary"),
        name="moe_dispatch",
    )(pstart.astype(jnp.int32), (pstart + counts).astype(jnp.int32), (padded - counts).astype(jnp.int32),
      n_used, _smem_rows(eid, tb), _smem_rows(rank, tb), h3)
    return xs, pstart.astype(jnp.int32), blk_expert, n_used


def _swiglu(gate, up):
    gate = jnp.minimum(gate, SWIGLU_LIMIT)
    up = jnp.clip(up, -SWIGLU_LIMIT, SWIGLU_LIMIT)
    return gate * _sigmoid(SWIGLU_ALPHA * gate) * (up + 1.0)


def _new_expert(be_ref, i):
    return jnp.logical_or(i == 0, be_ref[i] != be_ref[jnp.maximum(i - 1, 0)])


def _moe_up_kernel(be_ref, nu_ref, x_ref, wg_ref, wu_ref, bg_ref, bu_ref, o_ref, wg_bf, wu_bf, x_bf):
    i = pl.program_id(1)

    @pl.when(i < nu_ref[0])
    def _():
        @pl.when(_new_expert(be_ref, i))
        def _():
            wg_bf[...] = wg_ref[...].astype(BF16)
            wu_bf[...] = wu_ref[...].astype(BF16)

        for cc in range(D // LANE):
            x_bf[:, cc * LANE:(cc + 1) * LANE] = x_ref[:, cc, :].astype(BF16)
        x = x_bf[...]
        gate = _dot(x, wg_bf[...]) + bg_ref[...]
        up = _dot(x, wu_bf[...]) + bu_ref[...]
        o_ref[...] = _swiglu(gate, up).astype(o_ref.dtype)

    @pl.when(i >= nu_ref[0])
    def _():
        o_ref[...] = jnp.zeros_like(o_ref)


def _moe_down_kernel(be_ref, nu_ref, h_ref, wd_ref, bd_ref, o_ref, wd_bf):
    i = pl.program_id(1)

    @pl.when(i < nu_ref[0])
    def _():
        @pl.when(_new_expert(be_ref, i))
        def _():
            wd_bf[...] = wd_ref[...].astype(BF16)

        o_ref[...] = _dot(h_ref[...], wd_bf[...]) + bd_ref[...]

    @pl.when(i >= nu_ref[0])
    def _():
        o_ref[...] = jnp.zeros_like(o_ref)


def _moe_experts(xs, blk_expert, n_used, l, w_gate, b_gate, w_up, b_up, w_down, b_down):
    tm = MOE_TM
    x3spec = pl.BlockSpec((tm, D // LANE, LANE), lambda j, i, be, nu: (jnp.minimum(i, nu[0] - 1), 0, 0))
    row = lambda j, i, be, nu: jnp.minimum(i, nu[0] - 1)
    exp = lambda j, i, be, nu: be[jnp.minimum(i, nu[0] - 1)]
    wspec = lambda tn: pl.BlockSpec((None, None, D, tn), lambda j, i, be, nu: (l, exp(j, i, be, nu), 0, j))
    bspec = lambda tn: pl.BlockSpec((None, None, 1, tn), lambda j, i, be, nu: (l, exp(j, i, be, nu), 0, j))
    xspec = pl.BlockSpec((tm, D), lambda j, i, be, nu: (row(j, i, be, nu), 0))
    ospec = lambda tn: pl.BlockSpec((tm, tn), lambda j, i, be, nu: (i, j))
    b4 = lambda b: b.reshape(DEPTH, N_EXPERTS, 1, -1)
    tn = MOE_TN_UP
    hmid = pl.pallas_call(
        _moe_up_kernel,
        grid_spec=pltpu.PrefetchScalarGridSpec(
            num_scalar_prefetch=2, grid=(D_FF // tn, MOE_BLOCKS),
            in_specs=[x3spec, wspec(tn), wspec(tn), bspec(tn), bspec(tn)],
            out_specs=ospec(tn),
            scratch_shapes=[pltpu.VMEM((D, tn), BF16), pltpu.VMEM((D, tn), BF16), pltpu.VMEM((tm, D), BF16)]),
        out_shape=jax.ShapeDtypeStruct((MOE_ROWS, D_FF), BF16),
        compiler_params=_cparams("arbitrary", "arbitrary"),
        name="moe_up",
    )(blk_expert, n_used, xs, w_gate, w_up, b4(b_gate), b4(b_up))
    tn = MOE_TN_DOWN
    return pl.pallas_call(
        _moe_down_kernel,
        grid_spec=pltpu.PrefetchScalarGridSpec(
            num_scalar_prefetch=2, grid=(D // tn, MOE_BLOCKS),
            in_specs=[xspec, wspec(tn), bspec(tn)],
            out_specs=ospec(tn),
            scratch_shapes=[pltpu.VMEM((D_FF, tn), BF16)]),
        out_shape=jax.ShapeDtypeStruct((MOE_ROWS, D), F32),
        compiler_params=_cparams("arbitrary", "arbitrary"),
        name="moe_down",
    )(blk_expert, n_used, hmid, w_down, b4(b_down))


def _combine_kernel(pstart_ref, eid_ref, rank_ref, eidn_ref, rankn_ref, y_ref, gates_ref, x_ref, mods_ref,
                    g_ref, *rest, tb, nblk, mode, mod_row, emit_x):
    if emit_x:
        x2_ref, h_ref, buf, sem = rest
    else:
        h_ref, buf, sem = rest
    i = pl.program_id(0)

    def issue(eref, rref, slot):
        def body(r, c):
            for k in range(TOP_K):
                d = pstart_ref[eref[0, 0, k * tb + r]] + rref[0, 0, k * tb + r]
                pltpu.make_async_copy(y_ref.at[pl.ds(d, 1), :],
                                      buf.at[slot, k, pl.ds(r, 1), :], sem.at[slot]).start()
            return c
        lax.fori_loop(0, tb, body, 0)

    @pl.when(i == 0)
    def _():
        issue(eid_ref, rank_ref, 0)

    @pl.when(i + 1 < nblk)
    def _():
        issue(eidn_ref, rankn_ref, (i + 1) % 2)

    slot = i % 2
    for k in range(TOP_K):
        pltpu.make_async_copy(y_ref.at[pl.ds(0, tb), :], buf.at[slot, k], sem.at[slot]).wait()
    gates = gates_ref[...]
    moe = gates[:, 0:1] * buf[slot, 0]
    for k in range(1, TOP_K):
        moe = moe + gates[:, k:k + 1] * buf[slot, k]
    x, h = _resid_norm(x_ref[...], moe, mods_ref[...], g_ref[...],
                       has_resid=True, gate_row=5, mode=mode, mod_row=mod_row)
    if emit_x:
        x2_ref[...] = x
    h_ref[...] = h.astype(h_ref.dtype)


def _moe_combine(y, route, x, mods_l, g, *, mode, mods_next=None, h_dtype=BF16):
    tb = COMBINE_TB
    nblk = T // tb
    eid, gates, rank, pstart = route
    eid3, rank3 = _smem_rows(eid, tb), _smem_rows(rank, tb)
    emit_x = mode == "mod"
    if emit_x:
        mods = jnp.concatenate([mods_next[:, 0:2], mods_l[:, 2:]], axis=1)
    else:
        mods = mods_l
    row = pl.BlockSpec((tb, D), lambda i, *_: (i, 0))
    cur = lambda: pl.BlockSpec((1, 1, TOP_K * tb), lambda i, *_: (i, 0, 0), memory_space=pltpu.SMEM)
    nxt = lambda: pl.BlockSpec((1, 1, TOP_K * tb), lambda i, *_: (jnp.minimum(i + 1, nblk - 1), 0, 0),
                               memory_space=pltpu.SMEM)
    out_specs = ([row] if emit_x else []) + [row]
    out_shape = ([jax.ShapeDtypeStruct((T, D), F32)] if emit_x else []) + [jax.ShapeDtypeStruct((T, D), h_dtype)]
    return pl.pallas_call(
        functools.partial(_combine_kernel, tb=tb, nblk=nblk, mode=mode, mod_row=0, emit_x=emit_x),
        grid_spec=pltpu.PrefetchScalarGridSpec(
            num_scalar_prefetch=1, grid=(nblk,),
            in_specs=[cur(), cur(), nxt(), nxt(),
                      pl.BlockSpec(memory_space=pl.ANY),
                      pl.BlockSpec((tb, TOP_K), lambda i, *_: (i, 0)),
                      row,
                      pl.BlockSpec((None, 8, D), lambda i, *_: (_seg_of_block(i, tb), 0, 0)),
                      pl.BlockSpec((1, D), lambda i, *_: (0, 0))],
            out_specs=out_specs,
            scratch_shapes=[pltpu.VMEM((2, TOP_K, tb, D), F32), pltpu.SemaphoreType.DMA((2,))]),
        out_shape=out_shape,
        compiler_params=_cparams("arbitrary"),
        name="moe_combine",
    )(pstart, eid3, rank3, eid3, rank3, y, gates, x, mods, g.reshape(1, D))


def _moe(h3, logits, l, w_gate, b_gate, w_up, b_up, w_down, b_down):
    eid, gates, rank, counts = _route(logits)
    xs, pstart, blk_expert, n_used = _dispatch(h3, eid, rank, counts)
    y = _moe_experts(xs, blk_expert, n_used, l, w_gate, b_gate, w_up, b_up, w_down, b_down)
    return y, (eid, gates, rank, pstart)


def kernel(x_prompt, x_sample, state_hgrn, cache_diff_k, cache_diff_v, cache_mla_ckv, cache_mla_kr, c, c_ctx,
           norm_mix, norm_ffn, w_ada, b_ada, hgrn_lb, w_in_even, g_hgrn_out, conv_w, conv_b, conv_ln_g, conv_ln_b,
           w_out_even, w_in_odd, w_uq, w_ukv, g_q, g_kv, diff_lambda, g_sub, w_out_odd,
           w_router, b_router, w_gate, b_gate, w_up, b_up, w_down, b_down, norm_final):
    x = jnp.concatenate([x_prompt.reshape(T_CTX, D), x_sample.reshape(T_LAT, D)], axis=0)
    cvec = jnp.concatenate([c_ctx[None, :], c, jnp.zeros((8 - 1 - DEC_BATCH, D), F32)], axis=0)
    ada = _ada(cvec, w_ada, b_ada)
    mods = [jnp.pad(ada[l, :1 + DEC_BATCH].reshape(1 + DEC_BATCH, 6, D), ((0, 0), (0, 2), (0, 0)))
            for l in range(DEPTH)]
    lower_bounds = jnp.cumsum(jax.nn.softmax(hgrn_lb.astype(F32), axis=0), axis=0)
    moe_w = lambda: (w_gate, b_gate, w_up, b_up, w_down, b_down)

    (h,) = _rmod(x, None, mods[0], norm_mix[0], mode="mod", mod_row=0)
    y0 = _mm(h, w_in_even[0], name="in_even")
    s0 = state_hgrn[:, 0].reshape(DEC_BATCH * 2 * A_HEADS, A_DK, A_DK)
    lb = lower_bounds[0].reshape(1, A_QK)
    o_fw, s_fw = _hgrn_dir(y0, lb, s0, rev=False)
    o_bw, s_bw = _hgrn_dir(y0, lb, s0, rev=True)
    o_a = _hgrn_out(o_fw, o_bw, y0, g_hgrn_out[0])
    o_b = _conv(y0, conv_w[0], conv_b[0], conv_ln_g[0], conv_ln_b[0])
    o = _mm(jnp.concatenate([o_a, o_b], axis=-1), w_out_even[0], name="out_even")
    x, h3, logits = _rmod(x, o, mods[0], norm_ffn[0], gate_row=2, mode="mod", mod_row=3, rows3d=True,
                          router=(w_router, b_router, 0))
    y, route = _moe(h3, logits, 0, *moe_w())
    x, h = _moe_combine(y, route, x, mods[0], norm_mix[1], mode="mod", mods_next=mods[1])
    new_hgrn = jnp.stack([s_fw, s_bw], axis=1)[:, None]

    w_in = jnp.concatenate([w_in_odd[0], w_in_odd[0][:, ODD_IN - ROPE:]], axis=1)
    y1 = _mm(h, w_in, name="in_odd")
    cq = y1[:, 3 * C_W:3 * C_W + Q_RANK]
    ckv_raw = y1[:, 3 * C_W + Q_RANK:3 * C_W + Q_RANK + KV_RANK]
    perm = np.concatenate([np.arange(D_HEADS)[:, None] * (NOPE + ROPE) + np.arange(NOPE)[None, :],
                           np.arange(D_HEADS)[:, None] * (NOPE + ROPE) + NOPE + np.arange(ROPE)[None, :]],
                          axis=None)
    qm = _mm(cq, w_uq[0][:, perm], g=g_q[0], tm=1024, tn=D_HEADS * (NOPE + ROPE), name="uq")
    kv, ckv = _mm(ckv_raw, w_ukv[0], g=g_kv[0], emit_norm=True, out_dtype=BF16, tm=1024, tn=2048, name="ukv")
    kv_c = _mm(cache_mla_ckv[:, 0].reshape(DEC_BATCH * PAST, KV_RANK), w_ukv[0], out_dtype=BF16,
               tm=1024, tn=2048, name="ukv_cache")
    kr_c = cache_mla_kr[:, 0].reshape(DEC_BATCH * PAST, ROPE)
    kr_c = jnp.concatenate([kr_c, kr_c], axis=-1)
    lq = diff_lambda[0].astype(F32)
    lam_init = 0.8 - 0.6 * math.exp(-0.3 * 1)
    lam = (jnp.exp(jnp.sum(lq[0] * lq[1])) - jnp.exp(jnp.sum(lq[2] * lq[3])) + lam_init).reshape(1, 1)
    ck = cache_diff_k[:, 0].reshape(DEC_BATCH * PAST, C_W)
    cv = cache_diff_v[:, 0].reshape(DEC_BATCH * PAST, C_W)
    o_c = jnp.concatenate([_diff_attn(y1, lam, g_sub[0], lam_init),
                           _diff_attn(y1, lam, g_sub[0], lam_init, ck, cv)], axis=0)
    o_d = jnp.concatenate([_mla_attn(qm, kv, y1), _mla_attn(qm, kv, y1, kv_c, kr_c)], axis=0)
    o = _mm(jnp.concatenate([o_c, o_d], axis=-1), w_out_odd[0], name="out_odd")
    x, h3, logits = _rmod(x, o, mods[1], norm_ffn[1], gate_row=2, mode="mod", mod_row=3, rows3d=True,
                          router=(w_router, b_router, 1))
    y, route = _moe(h3, logits, 1, *moe_w())
    (yout,) = _moe_combine(y, route, x, mods[1], norm_final, mode="final", h_dtype=F32)

    ctx = lambda a, shape: a[:T_CTX].reshape(shape)
    return (yout[:T_CTX].reshape(BATCH, SEQ, D), yout[T_CTX:].reshape(DEC_BATCH, DEC_SEQ, D),
            new_hgrn,
            ctx(y1[:, C_W:2 * C_W], (BATCH, 1, SEQ, C_HEADS, 2 * C_DH)),
            ctx(y1[:, 2 * C_W:3 * C_W], (BATCH, 1, SEQ, C_HEADS, 2 * C_DH)),
            ctx(ckv, (BATCH, 1, SEQ, KV_RANK)),
            ctx(y1[:, ODD_IN - ROPE:ODD_IN], (BATCH, 1, SEQ, ROPE)))
```

```python
import functools
import math

import numpy as np
import jax
import jax.numpy as jnp
from jax import lax
from jax.experimental import pallas as pl
from jax.experimental.pallas import tpu as pltpu

F32 = jnp.float32
BF16 = jnp.bfloat16

D = 2048
BATCH, SEQ = 32, 256
DEC_BATCH, DEC_SEQ = 2, 1024
PAST = 512
DEPTH = 2
GRID_W = 64
T_CTX = BATCH * SEQ
T_LAT = DEC_BATCH * DEC_SEQ
T = T_CTX + T_LAT
A_HEADS, A_DK = 8, 128
A_QK = A_HEADS * A_DK
B_W = 1024
CONV_K = 31
C_HEADS, C_DH = 4, 128
C_W = C_HEADS * 2 * C_DH
D_HEADS = 8
Q_RANK, KV_RANK = 512, 256
NOPE, ROPE, V_DIM = 128, 64, 128
ROPE_BASE = 10000.0
N_EXPERTS, TOP_K = 32, 4
D_FF = 2048
SWIGLU_ALPHA, SWIGLU_LIMIT = 1.702, 7.0
EVEN_IN = 3 * A_QK + 2 * A_QK + 2 * B_W
ODD_IN = 3 * C_W + Q_RANK + KV_RANK + ROPE
EPS = 1e-6

LANE = 128
UNIT = 256
ROW_TILE = 256
MM_TM, MM_TN = 512, 1024
MOE_TM = 512
MOE_TN_UP = 1024
MOE_TN_DOWN = 1024
MOE_ROWS = T * TOP_K + N_EXPERTS * MOE_TM
MOE_BLOCKS = MOE_ROWS // MOE_TM
ROUTE_TB = 256
DISPATCH_TB = 256
COMBINE_TB = 128
VMEM_LIMIT = 60 * 1024 * 1024
HGRN_LEVELS = (16, 32, 64, 128, 256)
HGRN_CLAMP = 40.0


def _cparams(*sem):
    return pltpu.CompilerParams(dimension_semantics=sem, vmem_limit_bytes=VMEM_LIMIT)


def _dot(a, b):
    return lax.dot_general(a, b, (((1,), (0,)), ((), ())), preferred_element_type=F32)


def _dot_nt(a, b):
    return lax.dot_general(a, b, (((1,), (1,)), ((), ())), preferred_element_type=F32)


def _dot_tn(a, b):
    return lax.dot_general(a, b, (((0,), (0,)), ((), ())), preferred_element_type=F32)


def _sigmoid(x):
    return 1.0 / (1.0 + jnp.exp(-x))


def _silu(x):
    return x * _sigmoid(x)


def _rms(x, g):
    return x * lax.rsqrt(jnp.mean(x * x, axis=-1, keepdims=True) + EPS) * g


def _seg_of_block(i, rows):
    nctx = T_CTX // rows
    return jnp.where(i < nctx, 0, 1 + (i - nctx) // (DEC_SEQ // rows))


def _ada_kernel(c_ref, w_ref, b_ref, o_ref):
    a = _silu(c_ref[...]).astype(BF16)
    o_ref[...] = _dot(a, w_ref[...].astype(BF16)) + b_ref[...]


def _ada(cvec, w_ada, b_ada):
    tn = 1024
    return pl.pallas_call(
        _ada_kernel,
        grid=(DEPTH, 6 * D // tn),
        in_specs=[pl.BlockSpec((8, D), lambda l, j: (0, 0)),
                  pl.BlockSpec((None, D, tn), lambda l, j: (l, 0, j)),
                  pl.BlockSpec((None, 1, tn), lambda l, j: (l, 0, j))],
        out_specs=pl.BlockSpec((None, 8, tn), lambda l, j: (l, 0, j)),
        out_shape=jax.ShapeDtypeStruct((DEPTH, 8, 6 * D), F32),
        compiler_params=_cparams("arbitrary", "arbitrary"),
        name="ada",
    )(cvec, w_ada, b_ada.reshape(DEPTH, 1, 6 * D))


def _mm_kernel(*refs, n_x, rms, emit_norm):
    it = iter(refs)
    x_refs = [next(it) for _ in range(n_x)]
    w_ref = next(it)
    g_ref = next(it) if rms else None
    o_ref = next(it)
    n_ref = next(it) if emit_norm else None
    wbf = next(it)

    @pl.when(pl.program_id(1) == 0)
    def _():
        wbf[...] = w_ref[...].astype(BF16)

    acc = None
    k0 = 0
    for x_ref in x_refs:
        x = x_ref[...]
        if rms:
            x = _rms(x.astype(F32), g_ref[...])
            if emit_norm:
                n_ref[...] = x
        part = _dot(x.astype(BF16), wbf[k0:k0 + x.shape[1], :])
        acc = part if acc is None else acc + part
        k0 += x.shape[1]
    o_ref[...] = acc.astype(o_ref.dtype)


def _mm(xs, w, *, out_dtype=F32, tm=MM_TM, tn=MM_TN, g=None, emit_norm=False, n_cols=None, name="mm"):
    xs = xs if isinstance(xs, list) else [xs]
    xs = [x if isinstance(x, tuple) else (x, x.shape[1], 0) for x in xs]
    M = xs[0][0].shape[0]
    K = sum(wd for _, wd, _ in xs)
    N = w.shape[1] if n_cols is None else n_cols
    assert w.shape[0] == K and (g is None or len(xs) == 1)
    tn = min(tn, N)
    nj = pl.cdiv(N, tn)
    assert M % tm == 0 and (not emit_norm or nj == 1)
    in_specs = [pl.BlockSpec((tm, wd), functools.partial(lambda j, i, cb: (i, cb), cb=cb)) for _, wd, cb in xs]
    in_specs.append(pl.BlockSpec((K, tn), lambda j, i: (0, j)))
    args = [x for x, _, _ in xs] + [w]
    if g is not None:
        in_specs.append(pl.BlockSpec((1, K), lambda j, i: (0, 0)))
        args.append(g.reshape(1, K))
    out_specs = [pl.BlockSpec((tm, tn), lambda j, i: (i, j))]
    out_shape = [jax.ShapeDtypeStruct((M, N), out_dtype)]
    if emit_norm:
        out_specs.append(pl.BlockSpec((tm, K), lambda j, i: (i, 0)))
        out_shape.append(jax.ShapeDtypeStruct((M, K), F32))
    res = pl.pallas_call(
        functools.partial(_mm_kernel, n_x=len(xs), rms=g is not None, emit_norm=emit_norm),
        grid=(nj, M // tm),
        in_specs=in_specs, out_specs=out_specs, out_shape=out_shape,
        scratch_shapes=[pltpu.VMEM((K, tn), BF16)],
        compiler_params=_cparams("arbitrary", "arbitrary"),
        name=name,
    )(*args)
    return res if emit_norm else res[0]


def _resid_norm(x, o, mods, g, *, has_resid, gate_row, mode, mod_row):
    if has_resid:
        x = x + mods[gate_row:gate_row + 1, :] * o
    h = _rms(x, g)
    if mode == "mod":
        h = h * (1.0 + mods[mod_row + 1:mod_row + 2, :]) + mods[mod_row:mod_row + 1, :]
    return x, h


PACK_W = 2 * LANE
PACK_ROWS = D // PACK_W


def _store_packed_rows(ref, h):
    def bf16_bits(v):
        u = lax.bitcast_convert_type(v, jnp.uint32)
        return (u + jnp.uint32(0x7FFF) + ((u >> 16) & jnp.uint32(1))) >> 16
    for cc in range(PACK_ROWS):
        lo = bf16_bits(h[:, cc * PACK_W:cc * PACK_W + LANE])
        hi = bf16_bits(h[:, cc * PACK_W + LANE:(cc + 1) * PACK_W])
        ref[pl.ds(cc, h.shape[0], stride=PACK_ROWS), :] = lo | (hi << 16)


def _load_packed_rows(ref, dst, rows):
    for cc in range(PACK_ROWS):
        w = ref[pl.ds(cc, rows, stride=PACK_ROWS), :]
        dst[:, cc * PACK_W:cc * PACK_W + LANE] = lax.bitcast_convert_type(w << 16, F32).astype(BF16)
        dst[:, cc * PACK_W + LANE:(cc + 1) * PACK_W] = lax.bitcast_convert_type(
            w & jnp.uint32(0xFFFF0000), F32).astype(BF16)


def _rmod_kernel(*refs, has_resid, gate_row, mode, mod_row, emit_logits, rows3d, split_x):
    it = iter(refs)
    x_ref = next(it)
    xb_ref = next(it) if split_x else None
    o_ref = next(it) if has_resid else None
    mods_ref, g_ref = next(it), next(it)
    if emit_logits:
        wr_ref, br_ref = next(it), next(it)
    x1_ref = next(it) if has_resid else None
    h_ref = next(it)
    x = x_ref[...]
    if split_x:
        x = jnp.where(pl.program_id(0) < T_CTX // ROW_TILE, x, xb_ref[...])
    x, h = _resid_norm(x, o_ref[...] if has_resid else None, mods_ref[...], g_ref[...],
                       has_resid=has_resid, gate_row=gate_row, mode=mode, mod_row=mod_row)
    if has_resid:
        x1_ref[...] = x
    if rows3d:
        _store_packed_rows(h_ref, h)
    else:
        h_ref[...] = h.astype(h_ref.dtype)
    if emit_logits:
        lg_ref = next(it)
        lg_ref[...] = lax.dot_general(h, wr_ref[...], (((1,), (0,)), ((), ())),
                                      precision=lax.Precision.HIGHEST,
                                      preferred_element_type=F32) + br_ref[...]


def _rmod(x, o, mods, g, *, gate_row=0, mode="mod", mod_row=0, h_dtype=BF16, router=None, rows3d=False):
    tm = ROW_TILE
    has_resid = o is not None
    split_x = isinstance(x, tuple)
    row = pl.BlockSpec((tm, D), lambda i: (i, 0))
    if split_x:
        nctx = T_CTX // tm
        in_specs = [pl.BlockSpec((tm, D), lambda i: (jnp.minimum(i, nctx - 1), 0)),
                    pl.BlockSpec((tm, D), lambda i: (jnp.maximum(i - nctx, 0), 0))]
        args = list(x)
    else:
        in_specs, args = [row], [x]
    if has_resid:
        in_specs.append(row)
        args.append(o)
    in_specs += [pl.BlockSpec((None, 8, D), lambda i: (_seg_of_block(i, tm), 0, 0)),
                 pl.BlockSpec((1, D), lambda i: (0, 0))]
    args += [mods, g.reshape(1, D)]
    if router is not None:
        w_router, b_router, l = router
        in_specs += [pl.BlockSpec((None, D, N_EXPERTS), lambda i: (l, 0, 0)),
                     pl.BlockSpec((None, 1, N_EXPERTS), lambda i: (l, 0, 0))]
        args += [w_router, b_router.reshape(DEPTH, 1, N_EXPERTS)]
    out_specs, out_shape = [], []
    if has_resid:
        out_specs.append(row)
        out_shape.append(jax.ShapeDtypeStruct((T, D), F32))
    if rows3d:
        out_specs.append(pl.BlockSpec((tm * PACK_ROWS, LANE), lambda i: (i, 0)))
        out_shape.append(jax.ShapeDtypeStruct((T * PACK_ROWS, LANE), jnp.uint32))
    else:
        out_specs.append(row)
        out_shape.append(jax.ShapeDtypeStruct((T, D), h_dtype))
    if router is not None:
        out_specs.append(pl.BlockSpec((tm, N_EXPERTS), lambda i: (i, 0)))
        out_shape.append(jax.ShapeDtypeStruct((T, N_EXPERTS), F32))
    return pl.pallas_call(
        functools.partial(_rmod_kernel, has_resid=has_resid, gate_row=gate_row, mode=mode,
                          mod_row=mod_row, emit_logits=router is not None, rows3d=rows3d, split_x=split_x),
        grid=(T // tm,), in_specs=in_specs, out_specs=out_specs, out_shape=out_shape,
        compiler_params=_cparams("arbitrary"),
        name="rmod",
    )(*args)


def _hgrn_consts(rev):
    L = UNIT
    t = np.arange(L)[:, None]
    s = np.arange(L)[None, :]
    order = (s >= t) if rev else (s <= t)
    masks = [((t // HGRN_LEVELS[0]) == (s // HGRN_LEVELS[0])) & order]
    masks += [(t // b) == (s // b) for b in HGRN_LEVELS[1:]]
    return jnp.asarray(order, BF16), jnp.asarray(np.stack(masks), F32)


def _hgrn_kernel(q_ref, v_ref, f_ref, lb_ref, s0_ref, tri_ref, msk_ref, o_ref, so_ref, st, *, rev):
    L = UNIT
    n = pl.program_id(1)
    is_ctx = n < BATCH

    @pl.when(is_ctx)
    def _():
        st[...] = jnp.zeros_like(st)

    @pl.when(jnp.logical_and(n >= BATCH, (n - BATCH) % (DEC_SEQ // L) == 0))
    def _():
        st[...] = s0_ref[...].T

    q = _silu(q_ref[...])
    v = v_ref[...].astype(BF16)
    lb = lb_ref[...]
    f = lb + (1.0 - lb) * _sigmoid(f_ref[...])
    k = 1.0 - f
    logf = jnp.log(f)
    hi = logf.astype(BF16)
    r1 = logf - hi.astype(F32)
    mid = r1.astype(BF16)
    lo = (r1 - mid.astype(F32)).astype(BF16)
    tri = tri_ref[...]
    bcum = _dot(tri, hi) + _dot(tri, mid) + _dot(tri, lo)

    rowi = lax.broadcasted_iota(jnp.int32, (L, A_DK), 0)
    scores = jnp.zeros((L, L), F32)
    for li, b in enumerate(HGRN_LEVELS):
        r = b // 2 if (rev or li == 0) else b // 2 - 1
        b3 = bcum.reshape(L // b, b, A_DK)
        ref = jnp.broadcast_to(b3[:, r:r + 1, :], (L // b, b, A_DK)).reshape(L, A_DK)
        dlt = bcum - ref
        if li == 0:
            qt = q * jnp.exp(jnp.clip(dlt, -HGRN_CLAMP, HGRN_CLAMP))
            kt = k * jnp.exp(jnp.clip(-dlt, -HGRN_CLAMP, HGRN_CLAMP))
        else:
            later = (rowi % b) >= (b // 2)
            q_rows = jnp.logical_not(later) if rev else later
            e = jnp.exp(-jnp.abs(dlt))
            qt = jnp.where(q_rows, q * e, 0.0)
            kt = jnp.where(q_rows, 0.0, k * e)
        scores = scores + msk_ref[li] * _dot_nt(qt.astype(BF16), kt.astype(BF16))

    s_prev = st[...]
    o = _dot(scores.astype(BF16), v) + _dot_nt((q * jnp.exp(bcum)).astype(BF16), s_prev.astype(BF16))
    o_ref[...] = o
    b_end = bcum[0:1, :] if rev else bcum[L - 1:L, :]
    kk = (k * jnp.exp(b_end - bcum)).astype(BF16)
    s_new = s_prev * jnp.exp(b_end) + _dot_tn(v, kk)
    st[...] = s_new

    @pl.when(is_ctx)
    def _():
        so_ref[...] = s_new.T


def _hgrn_dir(y0, lb, s0, *, rev):
    L = UNIT
    per = DEC_SEQ // L
    nb = A_QK // LANE
    d = 1 if rev else 0
    tri, msk = _hgrn_consts(rev)

    def rb(n):
        if not rev:
            return n
        m = n - BATCH
        return jnp.where(n < BATCH, n, BATCH + (m // per) * per + (per - 1 - m % per))

    def s0_idx(h, n):
        b = jnp.clip((n - BATCH) // per, 0, DEC_BATCH - 1)
        return (b * 2 * A_HEADS + d * A_HEADS + h, 0, 0)

    o, so = pl.pallas_call(
        functools.partial(_hgrn_kernel, rev=rev),
        grid=(A_HEADS, T // L),
        in_specs=[pl.BlockSpec((L, LANE), lambda h, n: (rb(n), h)),
                  pl.BlockSpec((L, LANE), lambda h, n: (rb(n), nb + h)),
                  pl.BlockSpec((L, LANE), lambda h, n: (rb(n), (2 + d) * nb + h)),
                  pl.BlockSpec((1, LANE), lambda h, n: (0, h)),
                  pl.BlockSpec((None, A_DK, A_DK), s0_idx),
                  pl.BlockSpec((L, L), lambda h, n: (0, 0)),
                  pl.BlockSpec((len(HGRN_LEVELS), L, L), lambda h, n: (0, 0, 0))],
        out_specs=[pl.BlockSpec((L, LANE), lambda h, n: (rb(n), h)),
                   pl.BlockSpec((None, A_DK, A_DK),
                                lambda h, n: (jnp.minimum(n, BATCH - 1) * A_HEADS + h, 0, 0))],
        out_shape=[jax.ShapeDtypeStruct((T, A_QK), F32),
                   jax.ShapeDtypeStruct((BATCH * A_HEADS, A_DK, A_DK), F32)],
        scratch_shapes=[pltpu.VMEM((A_DK, A_DK), F32)],
        compiler_params=_cparams("arbitrary", "arbitrary"),
        name="hgrn_bwd" if rev else "hgrn_fwd",
    )(y0, y0, y0, lb, s0, tri, msk)
    return o, so.reshape(BATCH, A_HEADS, A_DK, A_DK)


def _hgrn_out_kernel(of_ref, ob_ref, g_ref, gn_ref, o_ref):
    o = _rms(of_ref[...] + ob_ref[...], gn_ref[...])
    o_ref[...] = (o * _silu(g_ref[...])).astype(o_ref.dtype)


def _hgrn_out(o_fw, o_bw, y0, g_out):
    tm = 1024
    nb = A_QK // LANE
    blk = lambda off: pl.BlockSpec((tm, LANE), lambda i, h: (i, off + h))
    return pl.pallas_call(
        _hgrn_out_kernel,
        grid=(T // tm, A_HEADS),
        in_specs=[blk(0), blk(0), blk(4 * nb), pl.BlockSpec((1, LANE), lambda i, h: (0, 0))],
        out_specs=blk(0),
        out_shape=jax.ShapeDtypeStruct((T, A_QK), BF16),
        compiler_params=_cparams("arbitrary", "arbitrary"),
        name="hgrn_out",
    )(o_fw, o_bw, y0, g_out.reshape(1, LANE))


CONV_HALO = 16
CONV_RC, CONV_CC = 32, 256


def _conv_kernel(ap_ref, gp_ref, a_ref, g_ref, an_ref, gn_ref, w_ref, b_ref, lg_ref, lb_ref,
                 o_ref, pad, yb):
    L = UNIT
    n = pl.program_id(0)
    s = (n - BATCH) % (DEC_SEQ // L)
    lat = n >= BATCH
    has_prev = jnp.logical_and(lat, s > 0).astype(F32)
    has_next = jnp.logical_and(lat, s < DEC_SEQ // L - 1).astype(F32)
    glu = lambda a, g: a[...] * _sigmoid(g[...])
    pad[0:CONV_HALO, :] = glu(ap_ref, gp_ref) * has_prev
    pad[CONV_HALO:CONV_HALO + L, :] = glu(a_ref, g_ref)
    pad[CONV_HALO + L:CONV_HALO + L + CONV_HALO, :] = glu(an_ref, gn_ref) * has_next
    base = CONV_HALO - CONV_K // 2
    for c0 in range(0, B_W, CONV_CC):
        w = w_ref[:, c0:c0 + CONV_CC]
        for r0 in range(0, L, CONV_RC):
            acc = jnp.zeros((CONV_RC, CONV_CC), F32)
            for kk in range(CONV_K):
                acc = acc + w[kk:kk + 1, :] * pad[r0 + base + kk:r0 + base + kk + CONV_RC, c0:c0 + CONV_CC]
            yb[r0:r0 + CONV_RC, c0:c0 + CONV_CC] = acc
    y = yb[...] + b_ref[...]
    mu = jnp.mean(y, axis=-1, keepdims=True)
    yc = y - mu
    yn = yc * lax.rsqrt(jnp.mean(yc * yc, axis=-1, keepdims=True) + EPS) * lg_ref[...] + lb_ref[...]
    o_ref[...] = _silu(yn).astype(o_ref.dtype)


def _conv(y0, conv_w, conv_b, ln_g, ln_b):
    L = UNIT
    ca = (3 * A_QK + 2 * A_QK) // B_W
    hb = L // CONV_HALO
    nhalo = T // CONV_HALO
    prev = lambda off: pl.BlockSpec((CONV_HALO, B_W), lambda n: (jnp.maximum(n * hb - 1, 0), off))
    cur = lambda off: pl.BlockSpec((L, B_W), lambda n: (n, off))
    nxt = lambda off: pl.BlockSpec((CONV_HALO, B_W), lambda n: (jnp.minimum((n + 1) * hb, nhalo - 1), off))
    vec = pl.BlockSpec((1, B_W), lambda n: (0, 0))
    return pl.pallas_call(
        _conv_kernel,
        grid=(T // L,),
        in_specs=[prev(ca), prev(ca + 1), cur(ca), cur(ca + 1), nxt(ca), nxt(ca + 1),
                  pl.BlockSpec((CONV_K, B_W), lambda n: (0, 0)), vec, vec, vec],
        out_specs=pl.BlockSpec((L, B_W), lambda n: (n, 0)),
        out_shape=jax.ShapeDtypeStruct((T, B_W), BF16),
        scratch_shapes=[pltpu.VMEM((L + 2 * CONV_HALO, B_W), F32), pltpu.VMEM((L, B_W), F32)],
        compiler_params=_cparams("arbitrary"),
        name="conv",
    )(y0, y0, y0, y0, y0, y0, conv_w, conv_b.reshape(1, B_W), ln_g.reshape(1, B_W), ln_b.reshape(1, B_W))


def _axial_tables(L, rot_dim):
    rows = L // GRID_W
    row = np.repeat(np.arange(rows), GRID_W).astype(np.float32)
    col = np.tile(np.arange(GRID_W), rows).astype(np.float32)
    quarter = rot_dim // 4
    inv = (ROPE_BASE ** (-np.arange(quarter, dtype=np.float32) / quarter)).astype(np.float32)
    ang = np.concatenate([row[:, None] * inv, col[:, None] * inv], axis=-1)
    return np.cos(ang).astype(np.float32), np.sin(ang).astype(np.float32)


def _rope_tables(L, rot_dim, reps):
    cos, sin = _axial_tables(L, rot_dim)
    c = np.tile(np.concatenate([cos, cos], axis=-1), (1, reps))
    s = np.tile(np.concatenate([-sin, sin], axis=-1), (1, reps))
    return jnp.asarray(c), jnp.asarray(s)


def _rope(x, c, s, rot_dim):
    w = x.shape[-1]
    half = rot_dim // 2
    if rot_dim == w:
        swapped = pltpu.roll(x, half, 1)
    else:
        lane = lax.broadcasted_iota(jnp.int32, x.shape, 1)
        swapped = jnp.where((lane % rot_dim) < half, pltpu.roll(x, w - half, 1), pltpu.roll(x, half, 1))
    return x * c + swapped * s


def _softmax_parts(scores):
    m = scores[0].max(axis=-1, keepdims=True)
    for s in scores[1:]:
        m = jnp.maximum(m, s.max(axis=-1, keepdims=True))
    es = [jnp.exp(s - m) for s in scores]
    den = es[0].sum(axis=-1, keepdims=True)
    for e in es[1:]:
        den = den + e.sum(axis=-1, keepdims=True)
    inv = 1.0 / den
    return [e * inv for e in es]


def _diff_kernel(*refs, two_seg, lam_init):
    it = iter(refs)
    lam_ref, q1_ref, q2_ref = next(it), next(it), next(it)
    if two_seg:
        k1c_ref, k2c_ref, vc_ref = next(it), next(it), next(it)
    k1_ref, k2_ref, v_ref = next(it), next(it), next(it)
    if two_seg:
        cq_ref, sq_ref, ck_ref, sk_ref = next(it), next(it), next(it), next(it)
    gs_ref, o_ref = next(it), next(it)
    scale = C_DH ** -0.5
    lam = lam_ref[0, 0]
    outs = []
    ps = []
    for q_ref, kc_ref, k_ref in ((q1_ref, k1c_ref if two_seg else None, k1_ref),
                                 (q2_ref, k2c_ref if two_seg else None, k2_ref)):
        q = q_ref[...]
        k = k_ref[...]
        if two_seg:
            sc = [_dot_nt(q.astype(BF16), kc_ref[...].astype(BF16)) * scale,
                  _dot_nt(_rope(q, cq_ref[...], sq_ref[...], C_DH).astype(BF16),
                          _rope(k, ck_ref[...], sk_ref[...], C_DH).astype(BF16)) * scale]
        else:
            sc = [_dot_nt(q.astype(BF16), k.astype(BF16)) * scale]
        ps.append(_softmax_parts(sc))
    vs = ([vc_ref[...]] if two_seg else []) + [v_ref[...]]
    o = None
    for p1, p2, vv in zip(ps[0], ps[1], vs):
        t = _dot((p1 - lam * p2).astype(BF16), vv.astype(BF16))
        o = t if o is None else o + t
    o_ref[...] = (_rms(o, gs_ref[...]) * (1.0 - lam_init)).astype(o_ref.dtype)


def _diff_attn(y1, lam, g_sub, lam_init, cache_k=None, cache_v=None):
    two_seg = cache_k is not None
    hw = 2 * C_DH
    smem = pl.BlockSpec(memory_space=pltpu.SMEM)
    if not two_seg:
        grid = (BATCH, C_HEADS)
        blk = lambda off, w: pl.BlockSpec((SEQ, w), lambda b, h: (b, off + h * (hw // w)))
        in_specs = [smem, blk(0, C_DH), blk(1, C_DH),
                    blk(C_W // C_DH, C_DH), blk(C_W // C_DH + 1, C_DH), blk(2 * C_W // hw, hw),
                    pl.BlockSpec((1, hw), lambda b, h: (0, 0))]
        args = [lam, y1, y1, y1, y1, y1, g_sub.reshape(1, hw)]
        out_specs = pl.BlockSpec((SEQ, hw), lambda b, h: (b, h))
        out_rows = T_CTX
        sem = ("arbitrary", "arbitrary")
    else:
        tq = 256
        nq = DEC_SEQ // tq
        r0 = T_CTX // DEC_SEQ
        grid = (DEC_BATCH, C_HEADS, nq)
        qblk = lambda off: pl.BlockSpec((tq, C_DH), lambda b, h, i: (T_CTX // tq + b * nq + i, off + 2 * h))
        kblk = lambda off, w: pl.BlockSpec((DEC_SEQ, w), lambda b, h, i: (r0 + b, off + h * (hw // w)))
        cblk = lambda off, w: pl.BlockSpec((PAST, w), lambda b, h, i: (b, off + h * (hw // w)))
        tq_blk = pl.BlockSpec((tq, C_DH), lambda b, h, i: (i, 0))
        tk_blk = pl.BlockSpec((DEC_SEQ, C_DH), lambda b, h, i: (0, 0))
        c, s = _rope_tables(DEC_SEQ, C_DH, 1)
        in_specs = [smem, qblk(0), qblk(1), cblk(0, C_DH), cblk(1, C_DH), cblk(0, hw),
                    kblk(C_W // C_DH, C_DH), kblk(C_W // C_DH + 1, C_DH), kblk(2 * C_W // hw, hw),
                    tq_blk, tq_blk, tk_blk, tk_blk,
                    pl.BlockSpec((1, hw), lambda b, h, i: (0, 0))]
        args = [lam, y1, y1, cache_k, cache_k, cache_v, y1, y1, y1, c, s, c, s, g_sub.reshape(1, hw)]
        out_specs = pl.BlockSpec((tq, hw), lambda b, h, i: (b * nq + i, h))
        out_rows = T_LAT
        sem = ("arbitrary", "arbitrary", "arbitrary")
    return pl.pallas_call(
        functools.partial(_diff_kernel, two_seg=two_seg, lam_init=lam_init),
        grid=grid, in_specs=in_specs, out_specs=out_specs,
        out_shape=jax.ShapeDtypeStruct((out_rows, C_W), BF16),
        compiler_params=_cparams(*sem),
        name="diff_lat" if two_seg else "diff_ctx",
    )(*args)


def _mla_kernel(*refs, two_seg):
    it = iter(refs)
    qn_ref, qr_ref = next(it), next(it)
    if two_seg:
        kvc_ref, krc_ref = next(it), next(it)
    kv_ref, kr_ref = next(it), next(it)
    if two_seg:
        cq_ref, sq_ref, ck_ref, sk_ref = next(it), next(it), next(it), next(it)
    o_ref = next(it)
    scale = (NOPE + ROPE) ** -0.5
    qr = qr_ref[...]
    kr = kr_ref[...]
    if two_seg:
        qr_rot = _rope(qr, cq_ref[...], sq_ref[...], ROPE)
        kr_rot = _rope(kr, ck_ref[...], sk_ref[...], ROPE)
        krc = krc_ref[...]
    lane = lax.broadcasted_iota(jnp.int32, (1, LANE), 1)
    for h in range(D_HEADS):
        half = (lane < ROPE) if h % 2 == 0 else (lane >= ROPE)
        pr = slice((h // 2) * LANE, (h // 2 + 1) * LANE)
        qn = qn_ref[:, h * NOPE:(h + 1) * NOPE]
        kn = kv_ref[:, h * 2 * NOPE:h * 2 * NOPE + NOPE]
        vv = kv_ref[:, h * 2 * NOPE + NOPE:(h + 1) * 2 * NOPE]
        cat = lambda a, b: jnp.concatenate([a.astype(BF16), b.astype(BF16)], axis=-1)
        if two_seg:
            knc = kvc_ref[:, h * 2 * NOPE:h * 2 * NOPE + NOPE]
            vc = kvc_ref[:, h * 2 * NOPE + NOPE:(h + 1) * 2 * NOPE]
            sc = [_dot_nt(cat(qn, qr[:, pr]), cat(knc, jnp.where(half, krc, 0.0))) * scale,
                  _dot_nt(cat(qn, qr_rot[:, pr]), cat(kn, jnp.where(half, kr_rot, 0.0))) * scale]
            p = _softmax_parts(sc)
            o = _dot(p[0].astype(BF16), vc) + _dot(p[1].astype(BF16), vv)
        else:
            sc = [_dot_nt(cat(qn, qr[:, pr]), cat(kn, jnp.where(half, kr, 0.0))) * scale]
            o = _dot(_softmax_parts(sc)[0].astype(BF16), vv)
        o_ref[:, h * V_DIM:(h + 1) * V_DIM] = o.astype(o_ref.dtype)


def _mla_attn(qm, kv, y1, kv_c=None, kr_c=None):
    two_seg = kv_c is not None
    qn_w, qr_w = D_HEADS * NOPE, D_HEADS * ROPE
    kr_col = (Q_RANK + KV_RANK) // LANE
    if not two_seg:
        grid = (BATCH,)
        in_specs = [pl.BlockSpec((SEQ, qn_w), lambda b: (b, 0)),
                    pl.BlockSpec((SEQ, qr_w), lambda b: (b, qn_w // qr_w)),
                    pl.BlockSpec((SEQ, 2 * qn_w), lambda b: (b, 0)),
                    pl.BlockSpec((SEQ, LANE), lambda b: (b, kr_col))]
        args = [qm, qm, kv, y1]
        out_specs = pl.BlockSpec((SEQ, qn_w), lambda b: (b, 0))
        out_rows = T_CTX
        sem = ("arbitrary",)
    else:
        tq = 256
        nq = DEC_SEQ // tq
        r0 = T_CTX // DEC_SEQ
        grid = (DEC_BATCH, nq)
        cq, sq = _rope_tables(DEC_SEQ, ROPE, qr_w // ROPE)
        ck, sk = _rope_tables(DEC_SEQ, ROPE, LANE // ROPE)
        in_specs = [pl.BlockSpec((tq, qn_w), lambda b, i: (T_CTX // tq + b * nq + i, 0)),
                    pl.BlockSpec((tq, qr_w), lambda b, i: (T_CTX // tq + b * nq + i, qn_w // qr_w)),
                    pl.BlockSpec((PAST, 2 * qn_w), lambda b, i: (b, 0)),
                    pl.BlockSpec((PAST, LANE), lambda b, i: (b, 0)),
                    pl.BlockSpec((DEC_SEQ, 2 * qn_w), lambda b, i: (r0 + b, 0)),
                    pl.BlockSpec((DEC_SEQ, LANE), lambda b, i: (r0 + b, kr_col)),
                    pl.BlockSpec((tq, qr_w), lambda b, i: (i, 0)),
                    pl.BlockSpec((tq, qr_w), lambda b, i: (i, 0)),
                    pl.BlockSpec((DEC_SEQ, LANE), lambda b, i: (0, 0)),
                    pl.BlockSpec((DEC_SEQ, LANE), lambda b, i: (0, 0))]
        args = [qm, qm, kv_c, kr_c, kv, y1, cq, sq, ck, sk]
        out_specs = pl.BlockSpec((tq, qn_w), lambda b, i: (b * nq + i, 0))
        out_rows = T_LAT
        sem = ("arbitrary", "arbitrary")
    return pl.pallas_call(
        functools.partial(_mla_kernel, two_seg=two_seg),
        grid=grid, in_specs=in_specs, out_specs=out_specs,
        out_shape=jax.ShapeDtypeStruct((out_rows, qn_w), BF16),
        compiler_params=_cparams(*sem),
        name="mla_lat" if two_seg else "mla_ctx",
    )(*args)


def _route_kernel(lg_ref, tril_ref, o_ref, cnt_ref, carry):
    tb = ROUTE_TB

    @pl.when(pl.program_id(0) == 0)
    def _():
        carry[...] = jnp.zeros_like(carry)

    l = lg_ref[...]
    lane = lax.broadcasted_iota(jnp.int32, (tb, N_EXPERTS), 1).astype(F32)
    vals, ids, sels = [], [], []
    for _ in range(TOP_K):
        m = l.max(axis=-1, keepdims=True)
        idx = jnp.min(jnp.where(l == m, lane, float(N_EXPERTS)), axis=-1, keepdims=True)
        sel = lane == idx
        vals.append(m)
        ids.append(idx)
        sels.append(sel)
        l = jnp.where(sel, -jnp.inf, l)
    es = [jnp.exp(v - vals[0]) for v in vals]
    den = es[0]
    for e in es[1:]:
        den = den + e
    inv = 1.0 / den
    picked = jnp.zeros((tb, N_EXPERTS), F32)
    for sel in sels:
        picked = picked + jnp.where(sel, 1.0, 0.0)
    base = carry[...] + _dot(tril_ref[...], picked.astype(BF16))
    carry[...] = carry[...] + jnp.sum(picked, axis=0, keepdims=True)
    cnt_ref[...] = carry[...]
    out_lane = lax.broadcasted_iota(jnp.int32, (tb, LANE), 1)
    out = jnp.zeros((tb, LANE), F32)
    for k in range(TOP_K):
        rank = jnp.sum(jnp.where(sels[k], base, 0.0), axis=-1, keepdims=True)
        out = jnp.where(out_lane == k, ids[k], out)
        out = jnp.where(out_lane == TOP_K + k, es[k] * inv, out)
        out = jnp.where(out_lane == 2 * TOP_K + k, rank, out)
    o_ref[...] = out


def _route(logits):
    tb = ROUTE_TB
    tril = jnp.asarray(np.tril(np.ones((tb, tb), np.float32), -1), BF16)
    packed, counts = pl.pallas_call(
        _route_kernel,
        grid=(T // tb,),
        in_specs=[pl.BlockSpec((tb, N_EXPERTS), lambda i: (i, 0)),
                  pl.BlockSpec((tb, tb), lambda i: (0, 0))],
        out_specs=[pl.BlockSpec((tb, LANE), lambda i: (i, 0)),
                   pl.BlockSpec((1, N_EXPERTS), lambda i: (0, 0))],
        out_shape=[jax.ShapeDtypeStruct((T, LANE), F32), jax.ShapeDtypeStruct((1, N_EXPERTS), F32)],
        scratch_shapes=[pltpu.VMEM((1, N_EXPERTS), F32)],
        compiler_params=_cparams("arbitrary"),
        name="moe_route",
    )(logits, tril)
    eid = packed[:, 0:TOP_K].astype(jnp.int32)
    gates = packed[:, TOP_K:2 * TOP_K]
    rank = packed[:, 2 * TOP_K:3 * TOP_K].astype(jnp.int32)
    return eid, gates, rank, counts[0].astype(jnp.int32)


def _smem_rows(a, tb):
    return a.reshape(T // tb, tb, TOP_K).transpose(0, 2, 1).reshape(T // tb, 1, TOP_K * tb)


def _dispatch_kernel(pstart_ref, padst_ref, padn_ref, nu_ref, eid_ref, rank_ref, h_ref, xs_ref, dest_ref,
                     zblk, sem, zsem):
    tb = DISPATCH_TB
    i = pl.program_id(0)

    def body(t, c):
        for k in range(TOP_K):
            d = pstart_ref[eid_ref[0, 0, k * tb + t]] + rank_ref[0, 0, k * tb + t]
            dest_ref[0, 0, k * tb + t] = d
            pltpu.make_async_copy(h_ref.at[t], xs_ref.at[d], sem).start()
        return c
    lax.fori_loop(0, tb, body, 0)

    def pad_rows(fn):
        def per_e(e, c):
            return lax.fori_loop(0, padn_ref[e], lambda r, c2: fn(padst_ref[e] + r, c2), c)
        lax.fori_loop(0, N_EXPERTS, per_e, 0)

    def pad_blocks(fn):
        lax.fori_loop(nu_ref[0], MOE_BLOCKS, fn, 0)

    row_copy = lambda d: pltpu.make_async_copy(zblk.at[0], xs_ref.at[d], zsem)
    blk_copy = lambda b: pltpu.make_async_copy(
        zblk, xs_ref.at[pl.ds(pl.multiple_of(b * MOE_TM, MOE_TM), MOE_TM)], zsem)

    @pl.when(i == 0)
    def _():
        zblk[...] = jnp.zeros_like(zblk)

        def start_row(d, c):
            row_copy(d).start()
            return c

        def start_blk(b, c):
            blk_copy(b).start()
            return c

        def wait_row(d, c):
            row_copy(d).wait()
            return c

        def wait_blk(b, c):
            blk_copy(b).wait()
            return c
        pad_rows(start_row)
        pad_blocks(start_blk)
        pad_rows(wait_row)
        pad_blocks(wait_blk)

    for k in range(TOP_K):
        pltpu.make_async_copy(h_ref, xs_ref.at[pl.ds(0, tb)], sem).wait()


def _dispatch(h3, eid, rank, counts):
    tb = DISPATCH_TB
    padded = (counts + MOE_TM - 1) // MOE_TM * MOE_TM
    pend = jnp.cumsum(padded)
    pstart = pend - padded
    n_used = (pend[-1:] // MOE_TM).astype(jnp.int32)
    blk_start = jnp.arange(MOE_BLOCKS, dtype=jnp.int32) * MOE_TM
    blk_expert = jnp.minimum(jnp.sum((pend[None, :] <= blk_start[:, None]).astype(jnp.int32), axis=1),
                             N_EXPERTS - 1).astype(jnp.int32)
    smem = lambda: pl.BlockSpec((1, 1, TOP_K * tb), lambda i, *_: (i, 0, 0), memory_space=pltpu.SMEM)
    xs, dest = pl.pallas_call(
        _dispatch_kernel,
        grid_spec=pltpu.PrefetchScalarGridSpec(
            num_scalar_prefetch=4, grid=(T // tb,),
            in_specs=[smem(), smem(), pl.BlockSpec((tb, PACK_ROWS, LANE), lambda i, *_: (i, 0, 0))],
            out_specs=[pl.BlockSpec(memory_space=pl.ANY), smem()],
            scratch_shapes=[pltpu.VMEM((MOE_TM, PACK_ROWS, LANE), jnp.uint32),
                            pltpu.SemaphoreType.DMA, pltpu.SemaphoreType.DMA]),
        out_shape=[jax.ShapeDtypeStruct((MOE_ROWS, PACK_ROWS, LANE), jnp.uint32),
                   jax.ShapeDtypeStruct((T // tb, 1, TOP_K * tb), jnp.int32)],
        compiler_params=_cparams("arbitrary"),
        name="moe_dispatch",
    )(pstart.astype(jnp.int32), (pstart + counts).astype(jnp.int32), (padded - counts).astype(jnp.int32),
      n_used, _smem_rows(eid, tb), _smem_rows(rank, tb), h3)
    return xs, dest, blk_expert, n_used


def _swiglu(gate, up):
    gate = jnp.minimum(gate, SWIGLU_LIMIT)
    up = jnp.clip(up, -SWIGLU_LIMIT, SWIGLU_LIMIT)
    return gate * _sigmoid(SWIGLU_ALPHA * gate) * (up + 1.0)


def _new_expert(be_ref, i):
    return jnp.logical_or(i == 0, be_ref[i] != be_ref[jnp.maximum(i - 1, 0)])


def _moe_up_kernel(be_ref, nu_ref, x_ref, wg_ref, wu_ref, bg_ref, bu_ref, o_ref, wg_bf, wu_bf, x_bf):
    i = pl.program_id(1)

    @pl.when(i < nu_ref[0])
    def _():
        @pl.when(_new_expert(be_ref, i))
        def _():
            wg_bf[...] = wg_ref[...].astype(BF16)
            wu_bf[...] = wu_ref[...].astype(BF16)

        _load_packed_rows(x_ref, x_bf, x_bf.shape[0])
        x = x_bf[...]
        gate = _dot(x, wg_bf[...]) + bg_ref[...]
        up = _dot(x, wu_bf[...]) + bu_ref[...]
        o_ref[...] = _swiglu(gate, up).astype(o_ref.dtype)

    @pl.when(i >= nu_ref[0])
    def _():
        o_ref[...] = jnp.zeros_like(o_ref)


def _moe_down_kernel(be_ref, nu_ref, h_ref, wd_ref, bd_ref, o_ref, wd_bf):
    i = pl.program_id(1)

    @pl.when(i < nu_ref[0])
    def _():
        @pl.when(_new_expert(be_ref, i))
        def _():
            wd_bf[...] = wd_ref[...].astype(BF16)

        o_ref[...] = _dot(h_ref[...], wd_bf[...]) + bd_ref[...]

    @pl.when(i >= nu_ref[0])
    def _():
        o_ref[...] = jnp.zeros_like(o_ref)


def _moe_experts(xs, blk_expert, n_used, l, w_gate, b_gate, w_up, b_up, w_down, b_down):
    tm = MOE_TM
    x3spec = pl.BlockSpec((tm * PACK_ROWS, LANE), lambda j, i, be, nu: (jnp.minimum(i, nu[0] - 1), 0))
    row = lambda j, i, be, nu: jnp.minimum(i, nu[0] - 1)
    exp = lambda j, i, be, nu: be[jnp.minimum(i, nu[0] - 1)]
    wspec = lambda tn: pl.BlockSpec((None, None, D, tn), lambda j, i, be, nu: (l, exp(j, i, be, nu), 0, j))
    bspec = lambda tn: pl.BlockSpec((None, None, 1, tn), lambda j, i, be, nu: (l, exp(j, i, be, nu), 0, j))
    xspec = pl.BlockSpec((tm, D), lambda j, i, be, nu: (row(j, i, be, nu), 0))
    ospec = lambda tn: pl.BlockSpec((tm, tn), lambda j, i, be, nu: (i, j))
    b4 = lambda b: b.reshape(DEPTH, N_EXPERTS, 1, -1)
    tn = MOE_TN_UP
    hmid = pl.pallas_call(
        _moe_up_kernel,
        grid_spec=pltpu.PrefetchScalarGridSpec(
            num_scalar_prefetch=2, grid=(D_FF // tn, MOE_BLOCKS),
            in_specs=[x3spec, wspec(tn), wspec(tn), bspec(tn), bspec(tn)],
            out_specs=ospec(tn),
            scratch_shapes=[pltpu.VMEM((D, tn), BF16), pltpu.VMEM((D, tn), BF16), pltpu.VMEM((tm, D), BF16)]),
        out_shape=jax.ShapeDtypeStruct((MOE_ROWS, D_FF), BF16),
        compiler_params=_cparams("arbitrary", "arbitrary"),
        name="moe_up",
    )(blk_expert, n_used, xs, w_gate, w_up, b4(b_gate), b4(b_up))
    tn = MOE_TN_DOWN
    return pl.pallas_call(
        _moe_down_kernel,
        grid_spec=pltpu.PrefetchScalarGridSpec(
            num_scalar_prefetch=2, grid=(D // tn, MOE_BLOCKS),
            in_specs=[xspec, wspec(tn), bspec(tn)],
            out_specs=ospec(tn),
            scratch_shapes=[pltpu.VMEM((D_FF, tn), BF16)]),
        out_shape=jax.ShapeDtypeStruct((MOE_ROWS, D), F32),
        compiler_params=_cparams("arbitrary", "arbitrary"),
        name="moe_down",
    )(blk_expert, n_used, hmid, w_down, b4(b_down))


def _combine_kernel(dest_ref, destn_ref, y_ref, gates_ref, x_ref, mods_ref, g_ref, *rest,
                    tb, nblk, mode, mod_row, emit_x):
    if emit_x:
        x2_ref, h_ref, buf, sem = rest
    else:
        hc_ref, hl_ref, buf, sem = rest
    i = pl.program_id(0)
    per = DISPATCH_TB // tb

    def issue(dref, step, slot):
        off = (step % per) * tb

        def body(r, c):
            for k in range(TOP_K):
                d = dref[0, 0, k * DISPATCH_TB + off + r]
                pltpu.make_async_copy(y_ref.at[pl.ds(d, 1), :],
                                      buf.at[slot, k, pl.ds(r, 1), :], sem.at[slot]).start()
            return c
        lax.fori_loop(0, tb, body, 0)

    @pl.when(i == 0)
    def _():
        issue(dest_ref, i, 0)

    @pl.when(i + 1 < nblk)
    def _():
        issue(destn_ref, i + 1, (i + 1) % 2)

    slot = i % 2
    for k in range(TOP_K):
        pltpu.make_async_copy(y_ref.at[pl.ds(0, tb), :], buf.at[slot, k], sem.at[slot]).wait()
    gates = gates_ref[...]
    moe = gates[:, 0:1] * buf[slot, 0]
    for k in range(1, TOP_K):
        moe = moe + gates[:, k:k + 1] * buf[slot, k]
    x, h = _resid_norm(x_ref[...], moe, mods_ref[...], g_ref[...],
                       has_resid=True, gate_row=5, mode=mode, mod_row=mod_row)
    if emit_x:
        x2_ref[...] = x
        h_ref[...] = h.astype(h_ref.dtype)
    else:
        @pl.when(i < T_CTX // tb)
        def _():
            hc_ref[...] = h

        @pl.when(i >= T_CTX // tb)
        def _():
            hl_ref[...] = h


def _moe_combine(y, route, x, mods_l, g, *, mode, mods_next=None):
    tb = COMBINE_TB
    nblk = T // tb
    per = DISPATCH_TB // tb
    dest, gates = route
    emit_x = mode == "mod"
    if emit_x:
        mods = jnp.concatenate([mods_next[:, 0:2], mods_l[:, 2:]], axis=1)
    else:
        mods = mods_l
    row = pl.BlockSpec((tb, D), lambda i: (i, 0))
    nctx = T_CTX // tb
    table = lambda step: pl.BlockSpec((1, 1, TOP_K * DISPATCH_TB), lambda i: (step(i) // per, 0, 0),
                                      memory_space=pltpu.SMEM)
    if emit_x:
        out_specs = [row, row]
        out_shape = [jax.ShapeDtypeStruct((T, D), F32), jax.ShapeDtypeStruct((T, D), BF16)]
    else:
        out_specs = [pl.BlockSpec((tb, D), lambda i: (jnp.minimum(i, nctx - 1), 0)),
                     pl.BlockSpec((tb, D), lambda i: (jnp.maximum(i - nctx, 0), 0))]
        out_shape = [jax.ShapeDtypeStruct((T_CTX, D), F32), jax.ShapeDtypeStruct((T_LAT, D), F32)]
    return pl.pallas_call(
        functools.partial(_combine_kernel, tb=tb, nblk=nblk, mode=mode, mod_row=0, emit_x=emit_x),
        grid=(nblk,),
        in_specs=[table(lambda i: i), table(lambda i: jnp.minimum(i + 1, nblk - 1)),
                  pl.BlockSpec(memory_space=pl.ANY),
                  pl.BlockSpec((tb, TOP_K), lambda i: (i, 0)),
                  row,
                  pl.BlockSpec((None, 8, D), lambda i: (_seg_of_block(i, tb), 0, 0)),
                  pl.BlockSpec((1, D), lambda i: (0, 0))],
        out_specs=out_specs, out_shape=out_shape,
        scratch_shapes=[pltpu.VMEM((2, TOP_K, tb, D), F32), pltpu.SemaphoreType.DMA((2,))],
        compiler_params=_cparams("arbitrary"),
        name="moe_combine",
    )(dest, dest, y, gates, x, mods, g.reshape(1, D))


def _moe(h3, logits, l, w_gate, b_gate, w_up, b_up, w_down, b_down):
    eid, gates, rank, counts = _route(logits)
    xs, dest, blk_expert, n_used = _dispatch(h3.reshape(T, PACK_ROWS, LANE), eid, rank, counts)
    y = _moe_experts(xs.reshape(MOE_ROWS * PACK_ROWS, LANE), blk_expert, n_used, l,
                     w_gate, b_gate, w_up, b_up, w_down, b_down)
    return y, (dest, gates)


def kernel(x_prompt, x_sample, state_hgrn, cache_diff_k, cache_diff_v, cache_mla_ckv, cache_mla_kr, c, c_ctx,
           norm_mix, norm_ffn, w_ada, b_ada, hgrn_lb, w_in_even, g_hgrn_out, conv_w, conv_b, conv_ln_g, conv_ln_b,
           w_out_even, w_in_odd, w_uq, w_ukv, g_q, g_kv, diff_lambda, g_sub, w_out_odd,
           w_router, b_router, w_gate, b_gate, w_up, b_up, w_down, b_down, norm_final):
    x = (x_prompt.reshape(T_CTX, D), x_sample.reshape(T_LAT, D))
    cvec = jnp.concatenate([c_ctx[None, :], c, jnp.zeros((8 - 1 - DEC_BATCH, D), F32)], axis=0)
    ada = _ada(cvec, w_ada, b_ada)
    mods = [jnp.pad(ada[l, :1 + DEC_BATCH].reshape(1 + DEC_BATCH, 6, D), ((0, 0), (0, 2), (0, 0)))
            for l in range(DEPTH)]
    lower_bounds = jnp.cumsum(jax.nn.softmax(hgrn_lb.astype(F32), axis=0), axis=0)
    moe_w = lambda: (w_gate, b_gate, w_up, b_up, w_down, b_down)

    (h,) = _rmod(x, None, mods[0], norm_mix[0], mode="mod", mod_row=0)
    y0 = _mm(h, w_in_even[0], name="in_even")
    s0 = state_hgrn[:, 0].reshape(DEC_BATCH * 2 * A_HEADS, A_DK, A_DK)
    lb = lower_bounds[0].reshape(1, A_QK)
    o_fw, s_fw = _hgrn_dir(y0, lb, s0, rev=False)
    o_bw, s_bw = _hgrn_dir(y0, lb, s0, rev=True)
    o_a = _hgrn_out(o_fw, o_bw, y0, g_hgrn_out[0])
    o_b = _conv(y0, conv_w[0], conv_b[0], conv_ln_g[0], conv_ln_b[0])
    o = _mm([o_a, o_b], w_out_even[0], name="out_even")
    x, h3, logits = _rmod(x, o, mods[0], norm_ffn[0], gate_row=2, mode="mod", mod_row=3, rows3d=True,
                          router=(w_router, b_router, 0))
    y, route = _moe(h3, logits, 0, *moe_w())
    x, h = _moe_combine(y, route, x, mods[0], norm_mix[1], mode="mod", mods_next=mods[1])
    new_hgrn = jnp.stack([s_fw, s_bw], axis=1)[:, None]

    y1 = _mm(h, w_in_odd[0], n_cols=3 * C_W, name="in_odd")
    w_tail = jnp.concatenate([w_in_odd[0][:, 3 * C_W:], w_in_odd[0][:, ODD_IN - ROPE:]], axis=1)
    y1t = _mm(h, w_tail, name="in_odd_tail")
    perm = np.concatenate([np.arange(D_HEADS)[:, None] * (NOPE + ROPE) + np.arange(NOPE)[None, :],
                           np.arange(D_HEADS)[:, None] * (NOPE + ROPE) + NOPE + np.arange(ROPE)[None, :]],
                          axis=None)
    qm = _mm((y1t, Q_RANK, 0), w_uq[0][:, perm], g=g_q[0], tm=1024, tn=D_HEADS * (NOPE + ROPE), name="uq")
    kv, ckv = _mm((y1t, KV_RANK, Q_RANK // KV_RANK), w_ukv[0], g=g_kv[0], emit_norm=True, out_dtype=BF16,
                  tm=1024, tn=2048, name="ukv")
    kv_c = _mm(cache_mla_ckv[:, 0].reshape(DEC_BATCH * PAST, KV_RANK), w_ukv[0], out_dtype=BF16,
               tm=1024, tn=2048, name="ukv_cache")
    kr_c = cache_mla_kr[:, 0].reshape(DEC_BATCH * PAST, ROPE)
    kr_c = jnp.concatenate([kr_c, kr_c], axis=-1)
    lq = diff_lambda[0].astype(F32)
    lam_init = 0.8 - 0.6 * math.exp(-0.3 * 1)
    lam = (jnp.exp(jnp.sum(lq[0] * lq[1])) - jnp.exp(jnp.sum(lq[2] * lq[3])) + lam_init).reshape(1, 1)
    ck = cache_diff_k[:, 0].reshape(DEC_BATCH * PAST, C_W)
    cv = cache_diff_v[:, 0].reshape(DEC_BATCH * PAST, C_W)
    o_c = jnp.concatenate([_diff_attn(y1, lam, g_sub[0], lam_init),
                           _diff_attn(y1, lam, g_sub[0], lam_init, ck, cv)], axis=0)
    o_d = jnp.concatenate([_mla_attn(qm, kv, y1t), _mla_attn(qm, kv, y1t, kv_c, kr_c)], axis=0)
    o = _mm([o_c, o_d], w_out_odd[0], name="out_odd")
    x, h3, logits = _rmod(x, o, mods[1], norm_ffn[1], gate_row=2, mode="mod", mod_row=3, rows3d=True,
                          router=(w_router, b_router, 1))
    y, route = _moe(h3, logits, 1, *moe_w())
    y_ctx, y_lat = _moe_combine(y, route, x, mods[1], norm_final, mode="final")

    ctx = lambda a, shape: a[:T_CTX].reshape(shape)
    kr0 = Q_RANK + KV_RANK
    return (y_ctx.reshape(BATCH, SEQ, D), y_lat.reshape(DEC_BATCH, DEC_SEQ, D),
            new_hgrn,
            ctx(y1[:, C_W:2 * C_W], (BATCH, 1, SEQ, C_HEADS, 2 * C_DH)),
            ctx(y1[:, 2 * C_W:3 * C_W], (BATCH, 1, SEQ, C_HEADS, 2 * C_DH)),
            ctx(ckv, (BATCH, 1, SEQ, KV_RANK)),
            ctx(y1t[:, kr0:kr0 + ROPE], (BATCH, 1, SEQ, ROPE)))
```

```python
import functools
import math

import numpy as np
import jax
import jax.numpy as jnp
from jax import lax
from jax.experimental import pallas as pl
from jax.experimental.pallas import tpu as pltpu

F32 = jnp.float32
BF16 = jnp.bfloat16

D = 2048
BATCH, SEQ = 32, 256
DEC_BATCH, DEC_SEQ = 2, 1024
PAST = 512
DEPTH = 2
GRID_W = 64
T_CTX = BATCH * SEQ
T_LAT = DEC_BATCH * DEC_SEQ
T = T_CTX + T_LAT
A_HEADS, A_DK = 8, 128
A_QK = A_HEADS * A_DK
B_W = 1024
CONV_K = 31
C_HEADS, C_DH = 4, 128
C_W = C_HEADS * 2 * C_DH
D_HEADS = 8
Q_RANK, KV_RANK = 512, 256
NOPE, ROPE, V_DIM = 128, 64, 128
ROPE_BASE = 10000.0
N_EXPERTS, TOP_K = 32, 4
D_FF = 2048
SWIGLU_ALPHA, SWIGLU_LIMIT = 1.702, 7.0
EVEN_IN = 3 * A_QK + 2 * A_QK + 2 * B_W
ODD_IN = 3 * C_W + Q_RANK + KV_RANK + ROPE
EPS = 1e-6

LANE = 128
UNIT = 256
ROW_TILE = 256
MM_TM, MM_TN = 512, 1024
MOE_TM = 512
MOE_TN_UP = 1024
MOE_TN_DOWN = 2048
MOE_ROWS = T * TOP_K + N_EXPERTS * MOE_TM
MOE_BLOCKS = MOE_ROWS // MOE_TM
ROUTE_TB = 256
DISPATCH_TB = 256
COMBINE_TB = 128
VMEM_LIMIT = 60 * 1024 * 1024
HGRN_LEVELS = (16, 32, 64, 128, 256)
HGRN_CLAMP = 40.0


def _cparams(*sem):
    return pltpu.CompilerParams(dimension_semantics=sem, vmem_limit_bytes=VMEM_LIMIT)


def _dot(a, b):
    return lax.dot_general(a, b, (((1,), (0,)), ((), ())), preferred_element_type=F32)


def _dot_nt(a, b):
    return lax.dot_general(a, b, (((1,), (1,)), ((), ())), preferred_element_type=F32)


def _dot_tn(a, b):
    return lax.dot_general(a, b, (((0,), (0,)), ((), ())), preferred_element_type=F32)


def _sigmoid(x):
    return 1.0 / (1.0 + jnp.exp(-x))


def _silu(x):
    return x * _sigmoid(x)


def _rms(x, g):
    return x * lax.rsqrt(jnp.mean(x * x, axis=-1, keepdims=True) + EPS) * g


def _seg_of_block(i, rows):
    nctx = T_CTX // rows
    return jnp.where(i < nctx, 0, 1 + (i - nctx) // (DEC_SEQ // rows))


def _ada_kernel(c_ref, w_ref, b_ref, o_ref):
    a = _silu(c_ref[...]).astype(BF16)
    o_ref[...] = _dot(a, w_ref[...].astype(BF16)) + b_ref[...]


def _ada(cvec, w_ada, b_ada):
    tn = 1024
    return pl.pallas_call(
        _ada_kernel,
        grid=(DEPTH, 6 * D // tn),
        in_specs=[pl.BlockSpec((8, D), lambda l, j: (0, 0)),
                  pl.BlockSpec((None, D, tn), lambda l, j: (l, 0, j)),
                  pl.BlockSpec((None, 1, tn), lambda l, j: (l, 0, j))],
        out_specs=pl.BlockSpec((None, 8, tn), lambda l, j: (l, 0, j)),
        out_shape=jax.ShapeDtypeStruct((DEPTH, 8, 6 * D), F32),
        compiler_params=_cparams("arbitrary", "arbitrary"),
        name="ada",
    )(cvec, w_ada, b_ada.reshape(DEPTH, 1, 6 * D))


def _mm_kernel(*refs, n_x, rms, emit_norm):
    it = iter(refs)
    x_refs = [next(it) for _ in range(n_x)]
    w_ref = next(it)
    g_ref = next(it) if rms else None
    o_ref = next(it)
    n_ref = next(it) if emit_norm else None
    wbf = next(it)

    @pl.when(pl.program_id(1) == 0)
    def _():
        wbf[...] = w_ref[...].astype(BF16)

    acc = None
    k0 = 0
    for x_ref in x_refs:
        x = x_ref[...]
        if rms:
            x = _rms(x.astype(F32), g_ref[...])
            if emit_norm:
                n_ref[...] = x
        part = _dot(x.astype(BF16), wbf[k0:k0 + x.shape[1], :])
        acc = part if acc is None else acc + part
        k0 += x.shape[1]
    o_ref[...] = acc.astype(o_ref.dtype)


def _mm(xs, w, *, out_dtype=F32, tm=MM_TM, tn=MM_TN, g=None, emit_norm=False, n_cols=None, name="mm"):
    xs = xs if isinstance(xs, list) else [xs]
    xs = [x if isinstance(x, tuple) else (x, x.shape[1], 0) for x in xs]
    M = xs[0][0].shape[0]
    K = sum(wd for _, wd, _ in xs)
    N = w.shape[1] if n_cols is None else n_cols
    assert w.shape[0] == K and (g is None or len(xs) == 1)
    tn = min(tn, N)
    nj = pl.cdiv(N, tn)
    assert M % tm == 0 and (not emit_norm or nj == 1)
    in_specs = [pl.BlockSpec((tm, wd), functools.partial(lambda j, i, cb: (i, cb), cb=cb)) for _, wd, cb in xs]
    in_specs.append(pl.BlockSpec((K, tn), lambda j, i: (0, j)))
    args = [x for x, _, _ in xs] + [w]
    if g is not None:
        in_specs.append(pl.BlockSpec((1, K), lambda j, i: (0, 0)))
        args.append(g.reshape(1, K))
    out_specs = [pl.BlockSpec((tm, tn), lambda j, i: (i, j))]
    out_shape = [jax.ShapeDtypeStruct((M, N), out_dtype)]
    if emit_norm:
        out_specs.append(pl.BlockSpec((tm, K), lambda j, i: (i, 0)))
        out_shape.append(jax.ShapeDtypeStruct((M, K), F32))
    res = pl.pallas_call(
        functools.partial(_mm_kernel, n_x=len(xs), rms=g is not None, emit_norm=emit_norm),
        grid=(nj, M // tm),
        in_specs=in_specs, out_specs=out_specs, out_shape=out_shape,
        scratch_shapes=[pltpu.VMEM((K, tn), BF16)],
        compiler_params=_cparams("arbitrary", "arbitrary"),
        name=name,
    )(*args)
    return res if emit_norm else res[0]


def _resid_norm(x, o, mods, g, *, has_resid, gate_row, mode, mod_row):
    if has_resid:
        x = x + mods[gate_row:gate_row + 1, :] * o
    h = _rms(x, g)
    if mode == "mod":
        h = h * (1.0 + mods[mod_row + 1:mod_row + 2, :]) + mods[mod_row:mod_row + 1, :]
    return x, h


PACK_ROWS = D // LANE


def _store_packed_rows(ref, h):
    for cc in range(PACK_ROWS):
        ref[pl.ds(cc, h.shape[0], stride=PACK_ROWS), :] = h[:, cc * LANE:(cc + 1) * LANE]


def _load_packed_rows(ref, dst, rows):
    for cc in range(PACK_ROWS):
        dst[:, cc * LANE:(cc + 1) * LANE] = ref[pl.ds(cc, rows, stride=PACK_ROWS), :].astype(BF16)


def _rmod_kernel(*refs, has_resid, gate_row, mode, mod_row, emit_logits, rows3d, split_x):
    it = iter(refs)
    x_ref = next(it)
    xb_ref = next(it) if split_x else None
    o_ref = next(it) if has_resid else None
    mods_ref, g_ref = next(it), next(it)
    if emit_logits:
        wr_ref, br_ref = next(it), next(it)
    x1_ref = next(it) if has_resid else None
    h_ref = next(it)
    x = x_ref[...]
    if split_x:
        x = jnp.where(pl.program_id(0) < T_CTX // ROW_TILE, x, xb_ref[...])
    x, h = _resid_norm(x, o_ref[...] if has_resid else None, mods_ref[...], g_ref[...],
                       has_resid=has_resid, gate_row=gate_row, mode=mode, mod_row=mod_row)
    if has_resid:
        x1_ref[...] = x
    if rows3d:
        _store_packed_rows(h_ref, h)
    else:
        h_ref[...] = h.astype(h_ref.dtype)
    if emit_logits:
        lg_ref = next(it)
        lg_ref[...] = lax.dot_general(h, wr_ref[...], (((1,), (0,)), ((), ())),
                                      precision=lax.Precision.HIGHEST,
                                      preferred_element_type=F32) + br_ref[...]


def _rmod(x, o, mods, g, *, gate_row=0, mode="mod", mod_row=0, h_dtype=BF16, router=None, rows3d=False):
    tm = ROW_TILE
    has_resid = o is not None
    split_x = isinstance(x, tuple)
    row = pl.BlockSpec((tm, D), lambda i: (i, 0))
    if split_x:
        nctx = T_CTX // tm
        in_specs = [pl.BlockSpec((tm, D), lambda i: (jnp.minimum(i, nctx - 1), 0)),
                    pl.BlockSpec((tm, D), lambda i: (jnp.maximum(i - nctx, 0), 0))]
        args = list(x)
    else:
        in_specs, args = [row], [x]
    if has_resid:
        in_specs.append(row)
        args.append(o)
    in_specs += [pl.BlockSpec((None, 8, D), lambda i: (_seg_of_block(i, tm), 0, 0)),
                 pl.BlockSpec((1, D), lambda i: (0, 0))]
    args += [mods, g.reshape(1, D)]
    if router is not None:
        w_router, b_router, l = router
        in_specs += [pl.BlockSpec((None, D, N_EXPERTS), lambda i: (l, 0, 0)),
                     pl.BlockSpec((None, 1, N_EXPERTS), lambda i: (l, 0, 0))]
        args += [w_router, b_router.reshape(DEPTH, 1, N_EXPERTS)]
    out_specs, out_shape = [], []
    if has_resid:
        out_specs.append(row)
        out_shape.append(jax.ShapeDtypeStruct((T, D), F32))
    if rows3d:
        out_specs.append(pl.BlockSpec((tm * PACK_ROWS, LANE), lambda i: (i, 0)))
        out_shape.append(jax.ShapeDtypeStruct((T * PACK_ROWS, LANE), F32))
    else:
        out_specs.append(row)
        out_shape.append(jax.ShapeDtypeStruct((T, D), h_dtype))
    if router is not None:
        out_specs.append(pl.BlockSpec((tm, N_EXPERTS), lambda i: (i, 0)))
        out_shape.append(jax.ShapeDtypeStruct((T, N_EXPERTS), F32))
    return pl.pallas_call(
        functools.partial(_rmod_kernel, has_resid=has_resid, gate_row=gate_row, mode=mode,
                          mod_row=mod_row, emit_logits=router is not None, rows3d=rows3d, split_x=split_x),
        grid=(T // tm,), in_specs=in_specs, out_specs=out_specs, out_shape=out_shape,
        compiler_params=_cparams("arbitrary"),
        name="rmod",
    )(*args)


def _hgrn_consts(rev):
    L = UNIT
    t = np.arange(L)[:, None]
    s = np.arange(L)[None, :]
    order = (s >= t) if rev else (s <= t)
    masks = [((t // HGRN_LEVELS[0]) == (s // HGRN_LEVELS[0])) & order]
    masks += [(t // b) == (s // b) for b in HGRN_LEVELS[1:]]
    return jnp.asarray(order, BF16), jnp.asarray(np.stack(masks), F32)


def _hgrn_kernel(q_ref, v_ref, f_ref, lb_ref, s0_ref, tri_ref, msk_ref, o_ref, so_ref, st, *, rev):
    L = UNIT
    n = pl.program_id(1)
    is_ctx = n < BATCH

    @pl.when(is_ctx)
    def _():
        st[...] = jnp.zeros_like(st)

    @pl.when(jnp.logical_and(n >= BATCH, (n - BATCH) % (DEC_SEQ // L) == 0))
    def _():
        st[...] = s0_ref[...].T

    q = _silu(q_ref[...])
    v = v_ref[...].astype(BF16)
    lb = lb_ref[...]
    f = lb + (1.0 - lb) * _sigmoid(f_ref[...])
    k = 1.0 - f
    logf = jnp.log(f)
    hi = logf.astype(BF16)
    r1 = logf - hi.astype(F32)
    mid = r1.astype(BF16)
    lo = (r1 - mid.astype(F32)).astype(BF16)
    tri = tri_ref[...]
    bcum = _dot(tri, hi) + _dot(tri, mid) + _dot(tri, lo)

    rowi = lax.broadcasted_iota(jnp.int32, (L, A_DK), 0)
    scores = jnp.zeros((L, L), F32)
    for li, b in enumerate(HGRN_LEVELS):
        r = b // 2 if (rev or li == 0) else b // 2 - 1
        b3 = bcum.reshape(L // b, b, A_DK)
        ref = jnp.broadcast_to(b3[:, r:r + 1, :], (L // b, b, A_DK)).reshape(L, A_DK)
        dlt = bcum - ref
        if li == 0:
            qt = q * jnp.exp(jnp.clip(dlt, -HGRN_CLAMP, HGRN_CLAMP))
            kt = k * jnp.exp(jnp.clip(-dlt, -HGRN_CLAMP, HGRN_CLAMP))
        else:
            later = (rowi % b) >= (b // 2)
            q_rows = jnp.logical_not(later) if rev else later
            e = jnp.exp(-jnp.abs(dlt))
            qt = jnp.where(q_rows, q * e, 0.0)
            kt = jnp.where(q_rows, 0.0, k * e)
        scores = scores + msk_ref[li] * _dot_nt(qt.astype(BF16), kt.astype(BF16))

    s_prev = st[...]
    o = _dot(scores.astype(BF16), v) + _dot_nt((q * jnp.exp(bcum)).astype(BF16), s_prev.astype(BF16))
    o_ref[...] = o
    b_end = bcum[0:1, :] if rev else bcum[L - 1:L, :]
    kk = (k * jnp.exp(b_end - bcum)).astype(BF16)
    s_new = s_prev * jnp.exp(b_end) + _dot_tn(v, kk)
    st[...] = s_new

    @pl.when(is_ctx)
    def _():
        so_ref[...] = s_new.T


def _hgrn_dir(y0, lb, s0, *, rev):
    L = UNIT
    per = DEC_SEQ // L
    nb = A_QK // LANE
    d = 1 if rev else 0
    tri, msk = _hgrn_consts(rev)

    def rb(n):
        if not rev:
            return n
        m = n - BATCH
        return jnp.where(n < BATCH, n, BATCH + (m // per) * per + (per - 1 - m % per))

    def s0_idx(h, n):
        b = jnp.clip((n - BATCH) // per, 0, DEC_BATCH - 1)
        return (b * 2 * A_HEADS + d * A_HEADS + h, 0, 0)

    o, so = pl.pallas_call(
        functools.partial(_hgrn_kernel, rev=rev),
        grid=(A_HEADS, T // L),
        in_specs=[pl.BlockSpec((L, LANE), lambda h, n: (rb(n), h)),
                  pl.BlockSpec((L, LANE), lambda h, n: (rb(n), nb + h)),
                  pl.BlockSpec((L, LANE), lambda h, n: (rb(n), (2 + d) * nb + h)),
                  pl.BlockSpec((1, LANE), lambda h, n: (0, h)),
                  pl.BlockSpec((None, A_DK, A_DK), s0_idx),
                  pl.BlockSpec((L, L), lambda h, n: (0, 0)),
                  pl.BlockSpec((len(HGRN_LEVELS), L, L), lambda h, n: (0, 0, 0))],
        out_specs=[pl.BlockSpec((L, LANE), lambda h, n: (rb(n), h)),
                   pl.BlockSpec((None, A_DK, A_DK),
                                lambda h, n: (jnp.minimum(n, BATCH - 1) * A_HEADS + h, 0, 0))],
        out_shape=[jax.ShapeDtypeStruct((T, A_QK), F32),
                   jax.ShapeDtypeStruct((BATCH * A_HEADS, A_DK, A_DK), F32)],
        scratch_shapes=[pltpu.VMEM((A_DK, A_DK), F32)],
        compiler_params=_cparams("arbitrary", "arbitrary"),
        name="hgrn_bwd" if rev else "hgrn_fwd",
    )(y0, y0, y0, lb, s0, tri, msk)
    return o, so.reshape(BATCH, A_HEADS, A_DK, A_DK)


def _hgrn_out_kernel(of_ref, ob_ref, g_ref, gn_ref, o_ref):
    o = _rms(of_ref[...] + ob_ref[...], gn_ref[...])
    o_ref[...] = (o * _silu(g_ref[...])).astype(o_ref.dtype)


def _hgrn_out(o_fw, o_bw, y0, g_out):
    tm = 1024
    nb = A_QK // LANE
    blk = lambda off: pl.BlockSpec((tm, LANE), lambda i, h: (i, off + h))
    return pl.pallas_call(
        _hgrn_out_kernel,
        grid=(T // tm, A_HEADS),
        in_specs=[blk(0), blk(0), blk(4 * nb), pl.BlockSpec((1, LANE), lambda i, h: (0, 0))],
        out_specs=blk(0),
        out_shape=jax.ShapeDtypeStruct((T, A_QK), BF16),
        compiler_params=_cparams("arbitrary", "arbitrary"),
        name="hgrn_out",
    )(o_fw, o_bw, y0, g_out.reshape(1, LANE))


CONV_HALO = 16
CONV_RC, CONV_CC = 32, 256


def _conv_kernel(ap_ref, gp_ref, a_ref, g_ref, an_ref, gn_ref, w_ref, b_ref, lg_ref, lb_ref,
                 o_ref, pad, sh, yb):
    L = UNIT
    SUB = 8
    span = L + CONV_HALO + SUB
    n = pl.program_id(0)
    s = (n - BATCH) % (DEC_SEQ // L)
    lat = n >= BATCH
    has_prev = jnp.logical_and(lat, s > 0).astype(F32)
    has_next = jnp.logical_and(lat, s < DEC_SEQ // L - 1).astype(F32)
    glu = lambda a, g: a[...] * _sigmoid(g[...])
    pad[0:CONV_HALO, :] = glu(ap_ref, gp_ref) * has_prev
    pad[CONV_HALO:CONV_HALO + L, :] = glu(a_ref, g_ref)
    pad[CONV_HALO + L:CONV_HALO + L + CONV_HALO, :] = glu(an_ref, gn_ref) * has_next
    base = CONV_HALO - CONV_K // 2
    for b in range(1, SUB):
        sh[b - 1, 0:span, :] = pad[b:b + span, :]
    for c0 in range(0, B_W, CONV_CC):
        w = w_ref[:, c0:c0 + CONV_CC]
        for r0 in range(0, L, CONV_RC):
            acc = jnp.zeros((CONV_RC, CONV_CC), F32)
            for kk in range(CONV_K):
                a, b = divmod(base + kk, SUB)
                lo = r0 + a * SUB
                if b == 0:
                    win = pad[lo:lo + CONV_RC, c0:c0 + CONV_CC]
                else:
                    win = sh[b - 1, lo:lo + CONV_RC, c0:c0 + CONV_CC]
                acc = acc + w[kk:kk + 1, :] * win
            yb[r0:r0 + CONV_RC, c0:c0 + CONV_CC] = acc
    y = yb[...] + b_ref[...]
    mu = jnp.mean(y, axis=-1, keepdims=True)
    yc = y - mu
    yn = yc * lax.rsqrt(jnp.mean(yc * yc, axis=-1, keepdims=True) + EPS) * lg_ref[...] + lb_ref[...]
    o_ref[...] = _silu(yn).astype(o_ref.dtype)


def _conv(y0, conv_w, conv_b, ln_g, ln_b):
    L = UNIT
    ca = (3 * A_QK + 2 * A_QK) // B_W
    hb = L // CONV_HALO
    nhalo = T // CONV_HALO
    prev = lambda off: pl.BlockSpec((CONV_HALO, B_W), lambda n: (jnp.maximum(n * hb - 1, 0), off))
    cur = lambda off: pl.BlockSpec((L, B_W), lambda n: (n, off))
    nxt = lambda off: pl.BlockSpec((CONV_HALO, B_W), lambda n: (jnp.minimum((n + 1) * hb, nhalo - 1), off))
    vec = pl.BlockSpec((1, B_W), lambda n: (0, 0))
    return pl.pallas_call(
        _conv_kernel,
        grid=(T // L,),
        in_specs=[prev(ca), prev(ca + 1), cur(ca), cur(ca + 1), nxt(ca), nxt(ca + 1),
                  pl.BlockSpec((CONV_K, B_W), lambda n: (0, 0)), vec, vec, vec],
        out_specs=pl.BlockSpec((L, B_W), lambda n: (n, 0)),
        out_shape=jax.ShapeDtypeStruct((T, B_W), BF16),
        scratch_shapes=[pltpu.VMEM((L + 2 * CONV_HALO, B_W), F32),
                        pltpu.VMEM((7, L + CONV_HALO + 8, B_W), F32), pltpu.VMEM((L, B_W), F32)],
        compiler_params=_cparams("arbitrary"),
        name="conv",
    )(y0, y0, y0, y0, y0, y0, conv_w, conv_b.reshape(1, B_W), ln_g.reshape(1, B_W), ln_b.reshape(1, B_W))


def _axial_tables(L, rot_dim):
    rows = L // GRID_W
    row = np.repeat(np.arange(rows), GRID_W).astype(np.float32)
    col = np.tile(np.arange(GRID_W), rows).astype(np.float32)
    quarter = rot_dim // 4
    inv = (ROPE_BASE ** (-np.arange(quarter, dtype=np.float32) / quarter)).astype(np.float32)
    ang = np.concatenate([row[:, None] * inv, col[:, None] * inv], axis=-1)
    return np.cos(ang).astype(np.float32), np.sin(ang).astype(np.float32)


def _rope_tables(L, rot_dim, reps):
    cos, sin = _axial_tables(L, rot_dim)
    c = np.tile(np.concatenate([cos, cos], axis=-1), (1, reps))
    s = np.tile(np.concatenate([-sin, sin], axis=-1), (1, reps))
    return jnp.asarray(c), jnp.asarray(s)


def _rope(x, c, s, rot_dim):
    w = x.shape[-1]
    half = rot_dim // 2
    if rot_dim == w:
        swapped = pltpu.roll(x, half, 1)
    else:
        lane = lax.broadcasted_iota(jnp.int32, x.shape, 1)
        swapped = jnp.where((lane % rot_dim) < half, pltpu.roll(x, w - half, 1), pltpu.roll(x, half, 1))
    return x * c + swapped * s


def _softmax_parts(scores):
    m = scores[0].max(axis=-1, keepdims=True)
    for s in scores[1:]:
        m = jnp.maximum(m, s.max(axis=-1, keepdims=True))
    es = [jnp.exp(s - m) for s in scores]
    den = es[0].sum(axis=-1, keepdims=True)
    for e in es[1:]:
        den = den + e.sum(axis=-1, keepdims=True)
    inv = 1.0 / den
    return [e * inv for e in es]


def _diff_kernel(*refs, two_seg, lam_init):
    it = iter(refs)
    lam_ref, q1_ref, q2_ref = next(it), next(it), next(it)
    if two_seg:
        k1c_ref, k2c_ref, vc_ref = next(it), next(it), next(it)
    k1_ref, k2_ref, v_ref = next(it), next(it), next(it)
    if two_seg:
        cq_ref, sq_ref, ck_ref, sk_ref = next(it), next(it), next(it), next(it)
    gs_ref, o_ref = next(it), next(it)
    scale = C_DH ** -0.5
    lam = lam_ref[0, 0]
    outs = []
    ps = []
    for q_ref, kc_ref, k_ref in ((q1_ref, k1c_ref if two_seg else None, k1_ref),
                                 (q2_ref, k2c_ref if two_seg else None, k2_ref)):
        q = q_ref[...]
        k = k_ref[...]
        if two_seg:
            sc = [_dot_nt(q.astype(BF16), kc_ref[...].astype(BF16)) * scale,
                  _dot_nt(_rope(q, cq_ref[...], sq_ref[...], C_DH).astype(BF16),
                          _rope(k, ck_ref[...], sk_ref[...], C_DH).astype(BF16)) * scale]
        else:
            sc = [_dot_nt(q.astype(BF16), k.astype(BF16)) * scale]
        ps.append(_softmax_parts(sc))
    vs = ([vc_ref[...]] if two_seg else []) + [v_ref[...]]
    o = None
    for p1, p2, vv in zip(ps[0], ps[1], vs):
        t = _dot((p1 - lam * p2).astype(BF16), vv.astype(BF16))
        o = t if o is None else o + t
    o_ref[...] = (_rms(o, gs_ref[...]) * (1.0 - lam_init)).astype(o_ref.dtype)


def _diff_attn(y1, lam, g_sub, lam_init, cache_k=None, cache_v=None):
    two_seg = cache_k is not None
    hw = 2 * C_DH
    smem = pl.BlockSpec(memory_space=pltpu.SMEM)
    if not two_seg:
        grid = (BATCH, C_HEADS)
        blk = lambda off, w: pl.BlockSpec((SEQ, w), lambda b, h: (b, off + h * (hw // w)))
        in_specs = [smem, blk(0, C_DH), blk(1, C_DH),
                    blk(C_W // C_DH, C_DH), blk(C_W // C_DH + 1, C_DH), blk(2 * C_W // hw, hw),
                    pl.BlockSpec((1, hw), lambda b, h: (0, 0))]
        args = [lam, y1, y1, y1, y1, y1, g_sub.reshape(1, hw)]
        out_specs = pl.BlockSpec((SEQ, hw), lambda b, h: (b, h))
        out_rows = T_CTX
        sem = ("arbitrary", "arbitrary")
    else:
        tq = 256
        nq = DEC_SEQ // tq
        r0 = T_CTX // DEC_SEQ
        grid = (DEC_BATCH, C_HEADS, nq)
        qblk = lambda off: pl.BlockSpec((tq, C_DH), lambda b, h, i: (T_CTX // tq + b * nq + i, off + 2 * h))
        kblk = lambda off, w: pl.BlockSpec((DEC_SEQ, w), lambda b, h, i: (r0 + b, off + h * (hw // w)))
        cblk = lambda off, w: pl.BlockSpec((PAST, w), lambda b, h, i: (b, off + h * (hw // w)))
        tq_blk = pl.BlockSpec((tq, C_DH), lambda b, h, i: (i, 0))
        tk_blk = pl.BlockSpec((DEC_SEQ, C_DH), lambda b, h, i: (0, 0))
        c, s = _rope_tables(DEC_SEQ, C_DH, 1)
        in_specs = [smem, qblk(0), qblk(1), cblk(0, C_DH), cblk(1, C_DH), cblk(0, hw),
                    kblk(C_W // C_DH, C_DH), kblk(C_W // C_DH + 1, C_DH), kblk(2 * C_W // hw, hw),
                    tq_blk, tq_blk, tk_blk, tk_blk,
                    pl.BlockSpec((1, hw), lambda b, h, i: (0, 0))]
        args = [lam, y1, y1, cache_k, cache_k, cache_v, y1, y1, y1, c, s, c, s, g_sub.reshape(1, hw)]
        out_specs = pl.BlockSpec((tq, hw), lambda b, h, i: (b * nq + i, h))
        out_rows = T_LAT
        sem = ("arbitrary", "arbitrary", "arbitrary")
    return pl.pallas_call(
        functools.partial(_diff_kernel, two_seg=two_seg, lam_init=lam_init),
        grid=grid, in_specs=in_specs, out_specs=out_specs,
        out_shape=jax.ShapeDtypeStruct((out_rows, C_W), BF16),
        compiler_params=_cparams(*sem),
        name="diff_lat" if two_seg else "diff_ctx",
    )(*args)


def _mla_kernel(*refs, two_seg):
    it = iter(refs)
    qn_ref, qr_ref = next(it), next(it)
    if two_seg:
        kvc_ref, krc_ref = next(it), next(it)
    kv_ref, kr_ref = next(it), next(it)
    if two_seg:
        cq_ref, sq_ref, ck_ref, sk_ref = next(it), next(it), next(it), next(it)
    o_ref = next(it)
    scale = (NOPE + ROPE) ** -0.5
    qr = qr_ref[...]
    kr = kr_ref[...]
    if two_seg:
        qr_rot = _rope(qr, cq_ref[...], sq_ref[...], ROPE)
        kr_rot = _rope(kr, ck_ref[...], sk_ref[...], ROPE)
        krc = krc_ref[...]
    lane = lax.broadcasted_iota(jnp.int32, (1, LANE), 1)
    for h in range(D_HEADS):
        half = (lane < ROPE) if h % 2 == 0 else (lane >= ROPE)
        pr = slice((h // 2) * LANE, (h // 2 + 1) * LANE)
        qn = qn_ref[:, h * NOPE:(h + 1) * NOPE]
        kn = kv_ref[:, h * 2 * NOPE:h * 2 * NOPE + NOPE]
        vv = kv_ref[:, h * 2 * NOPE + NOPE:(h + 1) * 2 * NOPE]
        cat = lambda a, b: jnp.concatenate([a.astype(BF16), b.astype(BF16)], axis=-1)
        if two_seg:
            knc = kvc_ref[:, h * 2 * NOPE:h * 2 * NOPE + NOPE]
            vc = kvc_ref[:, h * 2 * NOPE + NOPE:(h + 1) * 2 * NOPE]
            sc = [_dot_nt(cat(qn, qr[:, pr]), cat(knc, jnp.where(half, krc, 0.0))) * scale,
                  _dot_nt(cat(qn, qr_rot[:, pr]), cat(kn, jnp.where(half, kr_rot, 0.0))) * scale]
            p = _softmax_parts(sc)
            o = _dot(p[0].astype(BF16), vc) + _dot(p[1].astype(BF16), vv)
        else:
            sc = [_dot_nt(cat(qn, qr[:, pr]), cat(kn, jnp.where(half, kr, 0.0))) * scale]
            o = _dot(_softmax_parts(sc)[0].astype(BF16), vv)
        o_ref[:, h * V_DIM:(h + 1) * V_DIM] = o.astype(o_ref.dtype)


def _mla_attn(qm, kv, y1, kv_c=None, kr_c=None):
    two_seg = kv_c is not None
    qn_w, qr_w = D_HEADS * NOPE, D_HEADS * ROPE
    kr_col = (Q_RANK + KV_RANK) // LANE
    if not two_seg:
        grid = (BATCH,)
        in_specs = [pl.BlockSpec((SEQ, qn_w), lambda b: (b, 0)),
                    pl.BlockSpec((SEQ, qr_w), lambda b: (b, qn_w // qr_w)),
                    pl.BlockSpec((SEQ, 2 * qn_w), lambda b: (b, 0)),
                    pl.BlockSpec((SEQ, LANE), lambda b: (b, kr_col))]
        args = [qm, qm, kv, y1]
        out_specs = pl.BlockSpec((SEQ, qn_w), lambda b: (b, 0))
        out_rows = T_CTX
        sem = ("arbitrary",)
    else:
        tq = 256
        nq = DEC_SEQ // tq
        r0 = T_CTX // DEC_SEQ
        grid = (DEC_BATCH, nq)
        cq, sq = _rope_tables(DEC_SEQ, ROPE, qr_w // ROPE)
        ck, sk = _rope_tables(DEC_SEQ, ROPE, LANE // ROPE)
        in_specs = [pl.BlockSpec((tq, qn_w), lambda b, i: (T_CTX // tq + b * nq + i, 0)),
                    pl.BlockSpec((tq, qr_w), lambda b, i: (T_CTX // tq + b * nq + i, qn_w // qr_w)),
                    pl.BlockSpec((PAST, 2 * qn_w), lambda b, i: (b, 0)),
                    pl.BlockSpec((PAST, LANE), lambda b, i: (b, 0)),
                    pl.BlockSpec((DEC_SEQ, 2 * qn_w), lambda b, i: (r0 + b, 0)),
                    pl.BlockSpec((DEC_SEQ, LANE), lambda b, i: (r0 + b, kr_col)),
                    pl.BlockSpec((tq, qr_w), lambda b, i: (i, 0)),
                    pl.BlockSpec((tq, qr_w), lambda b, i: (i, 0)),
                    pl.BlockSpec((DEC_SEQ, LANE), lambda b, i: (0, 0)),
                    pl.BlockSpec((DEC_SEQ, LANE), lambda b, i: (0, 0))]
        args = [qm, qm, kv_c, kr_c, kv, y1, cq, sq, ck, sk]
        out_specs = pl.BlockSpec((tq, qn_w), lambda b, i: (b * nq + i, 0))
        out_rows = T_LAT
        sem = ("arbitrary", "arbitrary")
    return pl.pallas_call(
        functools.partial(_mla_kernel, two_seg=two_seg),
        grid=grid, in_specs=in_specs, out_specs=out_specs,
        out_shape=jax.ShapeDtypeStruct((out_rows, qn_w), BF16),
        compiler_params=_cparams(*sem),
        name="mla_lat" if two_seg else "mla_ctx",
    )(*args)


def _route_kernel(lg_ref, tril_ref, o_ref, cnt_ref, carry):
    tb = ROUTE_TB

    @pl.when(pl.program_id(0) == 0)
    def _():
        carry[...] = jnp.zeros_like(carry)

    l = lg_ref[...]
    lane = lax.broadcasted_iota(jnp.int32, (tb, N_EXPERTS), 1).astype(F32)
    vals, ids, sels = [], [], []
    for _ in range(TOP_K):
        m = l.max(axis=-1, keepdims=True)
        idx = jnp.min(jnp.where(l == m, lane, float(N_EXPERTS)), axis=-1, keepdims=True)
        sel = lane == idx
        vals.append(m)
        ids.append(idx)
        sels.append(sel)
        l = jnp.where(sel, -jnp.inf, l)
    es = [jnp.exp(v - vals[0]) for v in vals]
    den = es[0]
    for e in es[1:]:
        den = den + e
    inv = 1.0 / den
    picked = jnp.zeros((tb, N_EXPERTS), F32)
    for sel in sels:
        picked = picked + jnp.where(sel, 1.0, 0.0)
    base = carry[...] + _dot(tril_ref[...], picked.astype(BF16))
    carry[...] = carry[...] + jnp.sum(picked, axis=0, keepdims=True)
    cnt_ref[...] = carry[...]
    out_lane = lax.broadcasted_iota(jnp.int32, (tb, LANE), 1)
    out = jnp.zeros((tb, LANE), F32)
    for k in range(TOP_K):
        rank = jnp.sum(jnp.where(sels[k], base, 0.0), axis=-1, keepdims=True)
        out = jnp.where(out_lane == k, ids[k], out)
        out = jnp.where(out_lane == TOP_K + k, es[k] * inv, out)
        out = jnp.where(out_lane == 2 * TOP_K + k, rank, out)
    o_ref[...] = out


def _route(logits):
    tb = ROUTE_TB
    tril = jnp.asarray(np.tril(np.ones((tb, tb), np.float32), -1), BF16)
    packed, counts = pl.pallas_call(
        _route_kernel,
        grid=(T // tb,),
        in_specs=[pl.BlockSpec((tb, N_EXPERTS), lambda i: (i, 0)),
                  pl.BlockSpec((tb, tb), lambda i: (0, 0))],
        out_specs=[pl.BlockSpec((tb, LANE), lambda i: (i, 0)),
                   pl.BlockSpec((1, N_EXPERTS), lambda i: (0, 0))],
        out_shape=[jax.ShapeDtypeStruct((T, LANE), F32), jax.ShapeDtypeStruct((1, N_EXPERTS), F32)],
        scratch_shapes=[pltpu.VMEM((1, N_EXPERTS), F32)],
        compiler_params=_cparams("arbitrary"),
        name="moe_route",
    )(logits, tril)
    eid = packed[:, 0:TOP_K].astype(jnp.int32)
    gates = packed[:, TOP_K:2 * TOP_K]
    rank = packed[:, 2 * TOP_K:3 * TOP_K].astype(jnp.int32)
    return eid, gates, rank, counts[0].astype(jnp.int32)


def _smem_rows(a, tb):
    return a.reshape(T // tb, tb, TOP_K).transpose(0, 2, 1).reshape(T // tb, 1, TOP_K * tb)


def _dispatch_kernel(pstart_ref, padst_ref, padn_ref, nu_ref, eid_ref, rank_ref, h_ref, xs_ref, dest_ref,
                     zblk, sem, zsem):
    tb = DISPATCH_TB
    i = pl.program_id(0)

    def body(t, c):
        for k in range(TOP_K):
            d = pstart_ref[eid_ref[0, 0, k * tb + t]] + rank_ref[0, 0, k * tb + t]
            dest_ref[0, 0, k * tb + t] = d
            pltpu.make_async_copy(h_ref.at[t], xs_ref.at[d], sem).start()
        return c
    lax.fori_loop(0, tb, body, 0)

    def pad_rows(act):
        def per_e(e, c):
            n, d = padn_ref[e], padst_ref[e]
            p = MOE_TM // 2
            while p:
                @pl.when((n & p) != 0)
                def _(d=d, p=p):
                    act(pltpu.make_async_copy(zblk.at[pl.ds(0, p)], xs_ref.at[pl.ds(d, p)], zsem))
                d = d + (n & p)
                p //= 2
            return c
        lax.fori_loop(0, N_EXPERTS, per_e, 0)

    def pad_blocks(act):
        def per_b(b, c):
            act(pltpu.make_async_copy(zblk, xs_ref.at[pl.ds(pl.multiple_of(b * MOE_TM, MOE_TM), MOE_TM)], zsem))
            return c
        lax.fori_loop(nu_ref[0], MOE_BLOCKS, per_b, 0)

    @pl.when(i == 0)
    def _():
        zblk[...] = jnp.zeros_like(zblk)
        pad_rows(lambda cp: cp.start())
        pad_blocks(lambda cp: cp.start())
        pad_rows(lambda cp: cp.wait())
        pad_blocks(lambda cp: cp.wait())

    for k in range(TOP_K):
        pltpu.make_async_copy(h_ref, xs_ref.at[pl.ds(0, tb)], sem).wait()


def _dispatch(h3, eid, rank, counts):
    tb = DISPATCH_TB
    padded = (counts + MOE_TM - 1) // MOE_TM * MOE_TM
    pend = jnp.cumsum(padded)
    pstart = pend - padded
    n_used = (pend[-1:] // MOE_TM).astype(jnp.int32)
    blk_start = jnp.arange(MOE_BLOCKS, dtype=jnp.int32) * MOE_TM
    blk_expert = jnp.minimum(jnp.sum((pend[None, :] <= blk_start[:, None]).astype(jnp.int32), axis=1),
                             N_EXPERTS - 1).astype(jnp.int32)
    nxt_first = (pend // MOE_TM)[blk_expert].astype(jnp.int32)
    smem = lambda: pl.BlockSpec((1, 1, TOP_K * tb), lambda i, *_: (i, 0, 0), memory_space=pltpu.SMEM)
    xs, dest = pl.pallas_call(
        _dispatch_kernel,
        grid_spec=pltpu.PrefetchScalarGridSpec(
            num_scalar_prefetch=4, grid=(T // tb,),
            in_specs=[smem(), smem(), pl.BlockSpec((tb, PACK_ROWS, LANE), lambda i, *_: (i, 0, 0))],
            out_specs=[pl.BlockSpec(memory_space=pl.ANY), smem()],
            scratch_shapes=[pltpu.VMEM((MOE_TM, PACK_ROWS, LANE), F32),
                            pltpu.SemaphoreType.DMA, pltpu.SemaphoreType.DMA]),
        out_shape=[jax.ShapeDtypeStruct((MOE_ROWS, PACK_ROWS, LANE), F32),
                   jax.ShapeDtypeStruct((T // tb, 1, TOP_K * tb), jnp.int32)],
        compiler_params=_cparams("arbitrary"),
        name="moe_dispatch",
    )(pstart.astype(jnp.int32), (pstart + counts).astype(jnp.int32), (padded - counts).astype(jnp.int32),
      n_used, _smem_rows(eid, tb), _smem_rows(rank, tb), h3)
    return xs, dest, (blk_expert, n_used, nxt_first)


def _swiglu(gate, up):
    gate = jnp.minimum(gate, SWIGLU_LIMIT)
    up = jnp.clip(up, -SWIGLU_LIMIT, SWIGLU_LIMIT)
    return gate * _sigmoid(SWIGLU_ALPHA * gate) * (up + 1.0)


def _new_expert(be_ref, i):
    return jnp.logical_or(i == 0, be_ref[i] != be_ref[jnp.maximum(i - 1, 0)])


def _stream_weights(be_ref, nu_ref, nxt_ref, w_hbms, w_f32s, w_bfs, sems, *, l, tn):
    j, i = pl.program_id(0), pl.program_id(1)

    def copies(e, jj):
        col = pl.multiple_of(jj * tn, tn)
        return [pltpu.make_async_copy(w.at[l, e, :, pl.ds(col, tn)], buf, sems.at[k])
                for k, (w, buf) in enumerate(zip(w_hbms, w_f32s))]

    @pl.when(_new_expert(be_ref, i))
    def _():
        @pl.when(jnp.logical_and(i == 0, j == 0))
        def _():
            for cp in copies(be_ref[0], 0):
                cp.start()

        for cp in copies(be_ref[i], j):
            cp.wait()
        for buf, bf in zip(w_f32s, w_bfs):
            bf[...] = buf[...].astype(BF16)
        nf = nxt_ref[i]
        more = nf < nu_ref[0]

        @pl.when(more)
        def _():
            for cp in copies(be_ref[jnp.minimum(nf, MOE_BLOCKS - 1)], j):
                cp.start()

        @pl.when(jnp.logical_and(jnp.logical_not(more), j + 1 < pl.num_programs(0)))
        def _():
            for cp in copies(be_ref[0], j + 1):
                cp.start()


def _moe_up_kernel(be_ref, nu_ref, nxt_ref, x_ref, wg_hbm, wu_hbm, bg_ref, bu_ref, o_ref,
                   wg_f32, wu_f32, wg_bf, wu_bf, x_bf, sems, *, l, tn):
    i = pl.program_id(1)

    @pl.when(i < nu_ref[0])
    def _():
        _stream_weights(be_ref, nu_ref, nxt_ref, (wg_hbm, wu_hbm), (wg_f32, wu_f32), (wg_bf, wu_bf), sems,
                        l=l, tn=tn)
        _load_packed_rows(x_ref, x_bf, x_bf.shape[0])
        x = x_bf[...]
        gate = _dot(x, wg_bf[...]) + bg_ref[...]
        up = _dot(x, wu_bf[...]) + bu_ref[...]
        o_ref[...] = _swiglu(gate, up).astype(o_ref.dtype)

    @pl.when(i >= nu_ref[0])
    def _():
        o_ref[...] = jnp.zeros_like(o_ref)


def _moe_down_kernel(be_ref, nu_ref, nxt_ref, h_ref, wd_hbm, bd_ref, o_ref, wd_f32, wd_bf, sems, *, l, tn):
    i = pl.program_id(1)

    @pl.when(i < nu_ref[0])
    def _():
        _stream_weights(be_ref, nu_ref, nxt_ref, (wd_hbm,), (wd_f32,), (wd_bf,), sems, l=l, tn=tn)
        o_ref[...] = _dot(h_ref[...], wd_bf[...]) + bd_ref[...]

    @pl.when(i >= nu_ref[0])
    def _():
        o_ref[...] = jnp.zeros_like(o_ref)


def _moe_experts(xs, tables, l, w_gate, b_gate, w_up, b_up, w_down, b_down):
    tm = MOE_TM
    row = lambda j, i, be, nu, nx: jnp.minimum(i, nu[0] - 1)
    exp = lambda j, i, be, nu, nx: be[jnp.minimum(i, nu[0] - 1)]
    x3spec = pl.BlockSpec((tm * PACK_ROWS, LANE), lambda *a: (row(*a), 0))
    xspec = pl.BlockSpec((tm, D), lambda *a: (row(*a), 0))
    bspec = lambda tn: pl.BlockSpec((None, None, 1, tn), lambda j, i, be, nu, nx: (l, exp(j, i, be, nu, nx), 0, j))
    ospec = lambda tn: pl.BlockSpec((tm, tn), lambda j, i, be, nu, nx: (i, j))
    hbm = pl.BlockSpec(memory_space=pl.ANY)
    b4 = lambda b: b.reshape(DEPTH, N_EXPERTS, 1, -1)
    tn = MOE_TN_UP
    hmid = pl.pallas_call(
        functools.partial(_moe_up_kernel, l=l, tn=tn),
        grid_spec=pltpu.PrefetchScalarGridSpec(
            num_scalar_prefetch=3, grid=(D_FF // tn, MOE_BLOCKS),
            in_specs=[x3spec, hbm, hbm, bspec(tn), bspec(tn)],
            out_specs=ospec(tn),
            scratch_shapes=[pltpu.VMEM((D, tn), F32), pltpu.VMEM((D, tn), F32),
                            pltpu.VMEM((D, tn), BF16), pltpu.VMEM((D, tn), BF16),
                            pltpu.VMEM((tm, D), BF16), pltpu.SemaphoreType.DMA((2,))]),
        out_shape=jax.ShapeDtypeStruct((MOE_ROWS, D_FF), BF16),
        compiler_params=_cparams("arbitrary", "arbitrary"),
        name="moe_up",
    )(*tables, xs, w_gate, w_up, b4(b_gate), b4(b_up))
    tn = MOE_TN_DOWN
    return pl.pallas_call(
        functools.partial(_moe_down_kernel, l=l, tn=tn),
        grid_spec=pltpu.PrefetchScalarGridSpec(
            num_scalar_prefetch=3, grid=(D // tn, MOE_BLOCKS),
            in_specs=[xspec, hbm, bspec(tn)],
            out_specs=ospec(tn),
            scratch_shapes=[pltpu.VMEM((D_FF, tn), F32), pltpu.VMEM((D_FF, tn), BF16),
                            pltpu.SemaphoreType.DMA((1,))]),
        out_shape=jax.ShapeDtypeStruct((MOE_ROWS, D), F32),
        compiler_params=_cparams("arbitrary", "arbitrary"),
        name="moe_down",
    )(*tables, hmid, w_down, b4(b_down))


def _combine_kernel(dest_ref, destn_ref, y_ref, gates_ref, x_ref, mods_ref, g_ref, *rest,
                    tb, nblk, mode, mod_row, emit_x):
    if emit_x:
        x2_ref, h_ref, buf, sem = rest
    else:
        hc_ref, hl_ref, buf, sem = rest
    i = pl.program_id(0)
    per = DISPATCH_TB // tb

    def issue(dref, step, slot):
        off = (step % per) * tb

        def body(r, c):
            for k in range(TOP_K):
                d = dref[0, 0, k * DISPATCH_TB + off + r]
                pltpu.make_async_copy(y_ref.at[pl.ds(d, 1), :],
                                      buf.at[slot, k, pl.ds(r, 1), :], sem.at[slot]).start()
            return c
        lax.fori_loop(0, tb, body, 0)

    @pl.when(i == 0)
    def _():
        issue(dest_ref, i, 0)

    @pl.when(i + 1 < nblk)
    def _():
        issue(destn_ref, i + 1, (i + 1) % 2)

    slot = i % 2
    for k in range(TOP_K):
        pltpu.make_async_copy(y_ref.at[pl.ds(0, tb), :], buf.at[slot, k], sem.at[slot]).wait()
    gates = gates_ref[...]
    moe = gates[:, 0:1] * buf[slot, 0]
    for k in range(1, TOP_K):
        moe = moe + gates[:, k:k + 1] * buf[slot, k]
    x, h = _resid_norm(x_ref[...], moe, mods_ref[...], g_ref[...],
                       has_resid=True, gate_row=5, mode=mode, mod_row=mod_row)
    if emit_x:
        x2_ref[...] = x
        h_ref[...] = h.astype(h_ref.dtype)
    else:
        @pl.when(i < T_CTX // tb)
        def _():
            hc_ref[...] = h

        @pl.when(i >= T_CTX // tb)
        def _():
            hl_ref[...] = h


def _moe_combine(y, route, x, mods_l, g, *, mode, mods_next=None):
    tb = COMBINE_TB
    nblk = T // tb
    per = DISPATCH_TB // tb
    dest, gates = route
    emit_x = mode == "mod"
    if emit_x:
        mods = jnp.concatenate([mods_next[:, 0:2], mods_l[:, 2:]], axis=1)
    else:
        mods = mods_l
    row = pl.BlockSpec((tb, D), lambda i: (i, 0))
    nctx = T_CTX // tb
    table = lambda step: pl.BlockSpec((1, 1, TOP_K * DISPATCH_TB), lambda i: (step(i) // per, 0, 0),
                                      memory_space=pltpu.SMEM)
    if emit_x:
        out_specs = [row, row]
        out_shape = [jax.ShapeDtypeStruct((T, D), F32), jax.ShapeDtypeStruct((T, D), BF16)]
    else:
        out_specs = [pl.BlockSpec((tb, D), lambda i: (jnp.minimum(i, nctx - 1), 0)),
                     pl.BlockSpec((tb, D), lambda i: (jnp.maximum(i - nctx, 0), 0))]
        out_shape = [jax.ShapeDtypeStruct((T_CTX, D), F32), jax.ShapeDtypeStruct((T_LAT, D), F32)]
    return pl.pallas_call(
        functools.partial(_combine_kernel, tb=tb, nblk=nblk, mode=mode, mod_row=0, emit_x=emit_x),
        grid=(nblk,),
        in_specs=[table(lambda i: i), table(lambda i: jnp.minimum(i + 1, nblk - 1)),
                  pl.BlockSpec(memory_space=pl.ANY),
                  pl.BlockSpec((tb, TOP_K), lambda i: (i, 0)),
                  row,
                  pl.BlockSpec((None, 8, D), lambda i: (_seg_of_block(i, tb), 0, 0)),
                  pl.BlockSpec((1, D), lambda i: (0, 0))],
        out_specs=out_specs, out_shape=out_shape,
        scratch_shapes=[pltpu.VMEM((2, TOP_K, tb, D), F32), pltpu.SemaphoreType.DMA((2,))],
        compiler_params=_cparams("arbitrary"),
        name="moe_combine",
    )(dest, dest, y, gates, x, mods, g.reshape(1, D))


def _moe(h3, logits, l, w_gate, b_gate, w_up, b_up, w_down, b_down):
    eid, gates, rank, counts = _route(logits)
    xs, dest, tables = _dispatch(h3.reshape(T, PACK_ROWS, LANE), eid, rank, counts)
    y = _moe_experts(xs.reshape(MOE_ROWS * PACK_ROWS, LANE), tables, l,
                     w_gate, b_gate, w_up, b_up, w_down, b_down)
    return y, (dest, gates)


def kernel(x_prompt, x_sample, state_hgrn, cache_diff_k, cache_diff_v, cache_mla_ckv, cache_mla_kr, c, c_ctx,
           norm_mix, norm_ffn, w_ada, b_ada, hgrn_lb, w_in_even, g_hgrn_out, conv_w, conv_b, conv_ln_g, conv_ln_b,
           w_out_even, w_in_odd, w_uq, w_ukv, g_q, g_kv, diff_lambda, g_sub, w_out_odd,
           w_router, b_router, w_gate, b_gate, w_up, b_up, w_down, b_down, norm_final):
    x = (x_prompt.reshape(T_CTX, D), x_sample.reshape(T_LAT, D))
    cvec = jnp.concatenate([c_ctx[None, :], c, jnp.zeros((8 - 1 - DEC_BATCH, D), F32)], axis=0)
    ada = _ada(cvec, w_ada, b_ada)
    mods = [jnp.pad(ada[l, :1 + DEC_BATCH].reshape(1 + DEC_BATCH, 6, D), ((0, 0), (0, 2), (0, 0)))
            for l in range(DEPTH)]
    lower_bounds = jnp.cumsum(jax.nn.softmax(hgrn_lb.astype(F32), axis=0), axis=0)
    moe_w = lambda: (w_gate, b_gate, w_up, b_up, w_down, b_down)

    (h,) = _rmod(x, None, mods[0], norm_mix[0], mode="mod", mod_row=0)
    y0 = _mm(h, w_in_even[0], name="in_even")
    s0 = state_hgrn[:, 0].reshape(DEC_BATCH * 2 * A_HEADS, A_DK, A_DK)
    lb = lower_bounds[0].reshape(1, A_QK)
    o_fw, s_fw = _hgrn_dir(y0, lb, s0, rev=False)
    o_bw, s_bw = _hgrn_dir(y0, lb, s0, rev=True)
    o_a = _hgrn_out(o_fw, o_bw, y0, g_hgrn_out[0])
    o_b = _conv(y0, conv_w[0], conv_b[0], conv_ln_g[0], conv_ln_b[0])
    o = _mm([o_a, o_b], w_out_even[0], name="out_even")
    x, h3, logits = _rmod(x, o, mods[0], norm_ffn[0], gate_row=2, mode="mod", mod_row=3, rows3d=True,
                          router=(w_router, b_router, 0))
    y, route = _moe(h3, logits, 0, *moe_w())
    x, h = _moe_combine(y, route, x, mods[0], norm_mix[1], mode="mod", mods_next=mods[1])
    new_hgrn = jnp.stack([s_fw, s_bw], axis=1)[:, None]

    y1 = _mm(h, w_in_odd[0], n_cols=3 * C_W, name="in_odd")
    w_tail = jnp.concatenate([w_in_odd[0][:, 3 * C_W:], w_in_odd[0][:, ODD_IN - ROPE:]], axis=1)
    y1t = _mm(h, w_tail, name="in_odd_tail")
    perm = np.concatenate([np.arange(D_HEADS)[:, None] * (NOPE + ROPE) + np.arange(NOPE)[None, :],
                           np.arange(D_HEADS)[:, None] * (NOPE + ROPE) + NOPE + np.arange(ROPE)[None, :]],
                          axis=None)
    qm = _mm((y1t, Q_RANK, 0), w_uq[0][:, perm], g=g_q[0], tm=1024, tn=D_HEADS * (NOPE + ROPE), name="uq")
    kv, ckv = _mm((y1t, KV_RANK, Q_RANK // KV_RANK), w_ukv[0], g=g_kv[0], emit_norm=True, out_dtype=BF16,
                  tm=1024, tn=2048, name="ukv")
    kv_c = _mm(cache_mla_ckv[:, 0].reshape(DEC_BATCH * PAST, KV_RANK), w_ukv[0], out_dtype=BF16,
               tm=1024, tn=2048, name="ukv_cache")
    kr_c = cache_mla_kr[:, 0].reshape(DEC_BATCH * PAST, ROPE)
    kr_c = jnp.concatenate([kr_c, kr_c], axis=-1)
    lq = diff_lambda[0].astype(F32)
    lam_init = 0.8 - 0.6 * math.exp(-0.3 * 1)
    lam = (jnp.exp(jnp.sum(lq[0] * lq[1])) - jnp.exp(jnp.sum(lq[2] * lq[3])) + lam_init).reshape(1, 1)
    ck = cache_diff_k[:, 0].reshape(DEC_BATCH * PAST, C_W)
    cv = cache_diff_v[:, 0].reshape(DEC_BATCH * PAST, C_W)
    o_c = jnp.concatenate([_diff_attn(y1, lam, g_sub[0], lam_init),
                           _diff_attn(y1, lam, g_sub[0], lam_init, ck, cv)], axis=0)
    o_d = jnp.concatenate([_mla_attn(qm, kv, y1t), _mla_attn(qm, kv, y1t, kv_c, kr_c)], axis=0)
    o = _mm([o_c, o_d], w_out_odd[0], name="out_odd")
    x, h3, logits = _rmod(x, o, mods[1], norm_ffn[1], gate_row=2, mode="mod", mod_row=3, rows3d=True,
                          router=(w_router, b_router, 1))
    y, route = _moe(h3, logits, 1, *moe_w())
    y_ctx, y_lat = _moe_combine(y, route, x, mods[1], norm_final, mode="final")

    ctx = lambda a, shape: a[:T_CTX].reshape(shape)
    kr0 = Q_RANK + KV_RANK
    return (y_ctx.reshape(BATCH, SEQ, D), y_lat.reshape(DEC_BATCH, DEC_SEQ, D),
            new_hgrn,
            ctx(y1[:, C_W:2 * C_W], (BATCH, 1, SEQ, C_HEADS, 2 * C_DH)),
            ctx(y1[:, 2 * C_W:3 * C_W], (BATCH, 1, SEQ, C_HEADS, 2 * C_DH)),
            ctx(ckv, (BATCH, 1, SEQ, KV_RANK)),
            ctx(y1t[:, kr0:kr0 + ROPE], (BATCH, 1, SEQ, ROPE)))
```

```python
import functools
import math

import numpy as np
import jax
import jax.numpy as jnp
from jax import lax
from jax.experimental import pallas as pl
from jax.experimental.pallas import tpu as pltpu

F32 = jnp.float32
BF16 = jnp.bfloat16

D = 2048
BATCH, SEQ = 32, 256
DEC_BATCH, DEC_SEQ = 2, 1024
PAST = 512
DEPTH = 2
GRID_W = 64
T_CTX = BATCH * SEQ
T_LAT = DEC_BATCH * DEC_SEQ
T = T_CTX + T_LAT
A_HEADS, A_DK = 8, 128
A_QK = A_HEADS * A_DK
B_W = 1024
CONV_K = 31
C_HEADS, C_DH = 4, 128
C_W = C_HEADS * 2 * C_DH
D_HEADS = 8
Q_RANK, KV_RANK = 512, 256
NOPE, ROPE, V_DIM = 128, 64, 128
ROPE_BASE = 10000.0
N_EXPERTS, TOP_K = 32, 4
D_FF = 2048
SWIGLU_ALPHA, SWIGLU_LIMIT = 1.702, 7.0
EVEN_IN = 3 * A_QK + 2 * A_QK + 2 * B_W
ODD_IN = 3 * C_W + Q_RANK + KV_RANK + ROPE
EPS = 1e-6

LANE = 128
UNIT = 256
ROW_TILE = 256
MM_TM, MM_TN = 1024, 1024
MOE_TM = 512
MOE_TN_UP = 1024
MOE_TN_DOWN = 2048
MOE_ROWS = T * TOP_K + N_EXPERTS * MOE_TM
MOE_BLOCKS = MOE_ROWS // MOE_TM
ROUTE_TB = 256
DISPATCH_TB = 256
COMBINE_TB = 128
VMEM_LIMIT = 60 * 1024 * 1024
HGRN_LEVELS = (16, 32, 64, 128, 256)
HGRN_CLAMP = 40.0


def _cparams(*sem):
    return pltpu.CompilerParams(dimension_semantics=sem, vmem_limit_bytes=VMEM_LIMIT)


def _dot(a, b):
    return lax.dot_general(a, b, (((1,), (0,)), ((), ())), preferred_element_type=F32)


def _dot_nt(a, b):
    return lax.dot_general(a, b, (((1,), (1,)), ((), ())), preferred_element_type=F32)


def _dot_tn(a, b):
    return lax.dot_general(a, b, (((0,), (0,)), ((), ())), preferred_element_type=F32)


def _sigmoid(x):
    return 1.0 / (1.0 + jnp.exp(-x))


def _silu(x):
    return x * _sigmoid(x)


def _rms(x, g):
    return x * lax.rsqrt(jnp.mean(x * x, axis=-1, keepdims=True) + EPS) * g


def _seg_of_block(i, rows):
    nctx = T_CTX // rows
    return jnp.where(i < nctx, 0, 1 + (i - nctx) // (DEC_SEQ // rows))


def _ada_kernel(c_ref, w_ref, b_ref, o_ref):
    a = _silu(c_ref[...]).astype(BF16)
    o_ref[...] = _dot(a, w_ref[...].astype(BF16)) + b_ref[...]


def _ada(cvec, w_ada, b_ada):
    tn = 1024
    return pl.pallas_call(
        _ada_kernel,
        grid=(DEPTH, 6 * D // tn),
        in_specs=[pl.BlockSpec((8, D), lambda l, j: (0, 0)),
                  pl.BlockSpec((None, D, tn), lambda l, j: (l, 0, j)),
                  pl.BlockSpec((None, 1, tn), lambda l, j: (l, 0, j))],
        out_specs=pl.BlockSpec((None, 8, tn), lambda l, j: (l, 0, j)),
        out_shape=jax.ShapeDtypeStruct((DEPTH, 8, 6 * D), F32),
        compiler_params=_cparams("arbitrary", "arbitrary"),
        name="ada",
    )(cvec, w_ada, b_ada.reshape(DEPTH, 1, 6 * D))


def _mm_kernel(*refs, n_x, rms, emit_norm):
    it = iter(refs)
    x_refs = [next(it) for _ in range(n_x)]
    w_ref = next(it)
    g_ref = next(it) if rms else None
    o_ref = next(it)
    n_ref = next(it) if emit_norm else None
    wbf = next(it)

    @pl.when(pl.program_id(1) == 0)
    def _():
        wbf[...] = w_ref[...].astype(BF16)

    acc = None
    k0 = 0
    for x_ref in x_refs:
        x = x_ref[...]
        if rms:
            x = _rms(x.astype(F32), g_ref[...])
            if emit_norm:
                n_ref[...] = x
        part = _dot(x.astype(BF16), wbf[k0:k0 + x.shape[1], :])
        acc = part if acc is None else acc + part
        k0 += x.shape[1]
    o_ref[...] = acc.astype(o_ref.dtype)


def _mm(xs, w, *, out_dtype=F32, tm=MM_TM, tn=MM_TN, g=None, emit_norm=False, n_cols=None, name="mm"):
    xs = xs if isinstance(xs, list) else [xs]
    xs = [x if isinstance(x, tuple) else (x, x.shape[1], 0) for x in xs]
    M = xs[0][0].shape[0]
    K = sum(wd for _, wd, _ in xs)
    N = w.shape[1] if n_cols is None else n_cols
    assert w.shape[0] == K and (g is None or len(xs) == 1)
    tn = min(tn, N)
    nj = pl.cdiv(N, tn)
    assert M % tm == 0 and (not emit_norm or nj == 1)
    in_specs = [pl.BlockSpec((tm, wd), functools.partial(lambda j, i, cb: (i, cb), cb=cb)) for _, wd, cb in xs]
    in_specs.append(pl.BlockSpec((K, tn), lambda j, i: (0, j)))
    args = [x for x, _, _ in xs] + [w]
    if g is not None:
        in_specs.append(pl.BlockSpec((1, K), lambda j, i: (0, 0)))
        args.append(g.reshape(1, K))
    out_specs = [pl.BlockSpec((tm, tn), lambda j, i: (i, j))]
    out_shape = [jax.ShapeDtypeStruct((M, N), out_dtype)]
    if emit_norm:
        out_specs.append(pl.BlockSpec((tm, K), lambda j, i: (i, 0)))
        out_shape.append(jax.ShapeDtypeStruct((M, K), F32))
    res = pl.pallas_call(
        functools.partial(_mm_kernel, n_x=len(xs), rms=g is not None, emit_norm=emit_norm),
        grid=(nj, M // tm),
        in_specs=in_specs, out_specs=out_specs, out_shape=out_shape,
        scratch_shapes=[pltpu.VMEM((K, tn), BF16)],
        compiler_params=_cparams("arbitrary", "arbitrary"),
        name=name,
    )(*args)
    return res if emit_norm else res[0]


def _resid_norm(x, o, mods, g, *, has_resid, gate_row, mode, mod_row):
    if has_resid:
        x = x + mods[gate_row:gate_row + 1, :] * o
    h = _rms(x, g)
    if mode == "mod":
        h = h * (1.0 + mods[mod_row + 1:mod_row + 2, :]) + mods[mod_row:mod_row + 1, :]
    return x, h


PACK_ROWS = D // LANE


def _store_packed_rows(ref, h):
    for cc in range(PACK_ROWS):
        ref[pl.ds(cc, h.shape[0], stride=PACK_ROWS), :] = h[:, cc * LANE:(cc + 1) * LANE]


def _load_packed_rows(ref, dst, rows):
    for cc in range(PACK_ROWS):
        dst[:, cc * LANE:(cc + 1) * LANE] = ref[pl.ds(cc, rows, stride=PACK_ROWS), :].astype(BF16)


def _rmod_kernel(*refs, has_resid, gate_row, mode, mod_row, emit_logits, rows3d, split_x):
    it = iter(refs)
    x_ref = next(it)
    xb_ref = next(it) if split_x else None
    o_ref = next(it) if has_resid else None
    mods_ref, g_ref = next(it), next(it)
    if emit_logits:
        wr_ref, br_ref = next(it), next(it)
    x1_ref = next(it) if has_resid else None
    h_ref = next(it)
    x = x_ref[...]
    if split_x:
        x = jnp.where(pl.program_id(0) < T_CTX // ROW_TILE, x, xb_ref[...])
    x, h = _resid_norm(x, o_ref[...] if has_resid else None, mods_ref[...], g_ref[...],
                       has_resid=has_resid, gate_row=gate_row, mode=mode, mod_row=mod_row)
    if has_resid:
        x1_ref[...] = x
    if rows3d:
        _store_packed_rows(h_ref, h)
    else:
        h_ref[...] = h.astype(h_ref.dtype)
    if emit_logits:
        lg_ref = next(it)
        lg_ref[...] = lax.dot_general(h, wr_ref[...], (((1,), (0,)), ((), ())),
                                      precision=lax.Precision.HIGHEST,
                                      preferred_element_type=F32) + br_ref[...]


def _rmod(x, o, mods, g, *, gate_row=0, mode="mod", mod_row=0, h_dtype=BF16, router=None, rows3d=False):
    tm = ROW_TILE
    has_resid = o is not None
    split_x = isinstance(x, tuple)
    row = pl.BlockSpec((tm, D), lambda i: (i, 0))
    if split_x:
        nctx = T_CTX // tm
        in_specs = [pl.BlockSpec((tm, D), lambda i: (jnp.minimum(i, nctx - 1), 0)),
                    pl.BlockSpec((tm, D), lambda i: (jnp.maximum(i - nctx, 0), 0))]
        args = list(x)
    else:
        in_specs, args = [row], [x]
    if has_resid:
        in_specs.append(row)
        args.append(o)
    in_specs += [pl.BlockSpec((None, 8, D), lambda i: (_seg_of_block(i, tm), 0, 0)),
                 pl.BlockSpec((1, D), lambda i: (0, 0))]
    args += [mods, g.reshape(1, D)]
    if router is not None:
        w_router, b_router, l = router
        in_specs += [pl.BlockSpec((None, D, N_EXPERTS), lambda i: (l, 0, 0)),
                     pl.BlockSpec((None, 1, N_EXPERTS), lambda i: (l, 0, 0))]
        args += [w_router, b_router.reshape(DEPTH, 1, N_EXPERTS)]
    out_specs, out_shape = [], []
    if has_resid:
        out_specs.append(row)
        out_shape.append(jax.ShapeDtypeStruct((T, D), F32))
    if rows3d:
        out_specs.append(pl.BlockSpec((tm * PACK_ROWS, LANE), lambda i: (i, 0)))
        out_shape.append(jax.ShapeDtypeStruct((T * PACK_ROWS, LANE), F32))
    else:
        out_specs.append(row)
        out_shape.append(jax.ShapeDtypeStruct((T, D), h_dtype))
    if router is not None:
        out_specs.append(pl.BlockSpec((tm, N_EXPERTS), lambda i: (i, 0)))
        out_shape.append(jax.ShapeDtypeStruct((T, N_EXPERTS), F32))
    return pl.pallas_call(
        functools.partial(_rmod_kernel, has_resid=has_resid, gate_row=gate_row, mode=mode,
                          mod_row=mod_row, emit_logits=router is not None, rows3d=rows3d, split_x=split_x),
        grid=(T // tm,), in_specs=in_specs, out_specs=out_specs, out_shape=out_shape,
        compiler_params=_cparams("arbitrary"),
        name="rmod",
    )(*args)


def _hgrn_consts(rev):
    L = UNIT
    t = np.arange(L)[:, None]
    s = np.arange(L)[None, :]
    order = (s >= t) if rev else (s <= t)
    masks = [((t // HGRN_LEVELS[0]) == (s // HGRN_LEVELS[0])) & order]
    masks += [(t // b) == (s // b) for b in HGRN_LEVELS[1:]]
    return jnp.asarray(order, BF16), jnp.asarray(np.stack(masks), F32)


def _hgrn_kernel(q_ref, v_ref, f_ref, lb_ref, s0_ref, tri_ref, msk_ref, o_ref, so_ref, st, *, rev):
    L = UNIT
    n = pl.program_id(1)
    is_ctx = n < BATCH

    @pl.when(is_ctx)
    def _():
        st[...] = jnp.zeros_like(st)

    @pl.when(jnp.logical_and(n >= BATCH, (n - BATCH) % (DEC_SEQ // L) == 0))
    def _():
        st[...] = s0_ref[...].T

    q = _silu(q_ref[...])
    v = v_ref[...].astype(BF16)
    lb = lb_ref[...]
    f = lb + (1.0 - lb) * _sigmoid(f_ref[...])
    k = 1.0 - f
    logf = jnp.log(f)
    hi = logf.astype(BF16)
    r1 = logf - hi.astype(F32)
    mid = r1.astype(BF16)
    lo = (r1 - mid.astype(F32)).astype(BF16)
    tri = tri_ref[...]
    bcum = _dot(tri, hi) + _dot(tri, mid) + _dot(tri, lo)

    rowi = lax.broadcasted_iota(jnp.int32, (L, A_DK), 0)
    scores = jnp.zeros((L, L), F32)
    for li, b in enumerate(HGRN_LEVELS):
        r = b // 2 if (rev or li == 0) else b // 2 - 1
        b3 = bcum.reshape(L // b, b, A_DK)
        ref = jnp.broadcast_to(b3[:, r:r + 1, :], (L // b, b, A_DK)).reshape(L, A_DK)
        dlt = bcum - ref
        if li == 0:
            qt = q * jnp.exp(jnp.clip(dlt, -HGRN_CLAMP, HGRN_CLAMP))
            kt = k * jnp.exp(jnp.clip(-dlt, -HGRN_CLAMP, HGRN_CLAMP))
        else:
            later = (rowi % b) >= (b // 2)
            q_rows = jnp.logical_not(later) if rev else later
            e = jnp.exp(-jnp.abs(dlt))
            qt = jnp.where(q_rows, q * e, 0.0)
            kt = jnp.where(q_rows, 0.0, k * e)
        scores = scores + msk_ref[li] * _dot_nt(qt.astype(BF16), kt.astype(BF16))

    s_prev = st[...]
    o = _dot(scores.astype(BF16), v) + _dot_nt((q * jnp.exp(bcum)).astype(BF16), s_prev.astype(BF16))
    o_ref[...] = o
    b_end = bcum[0:1, :] if rev else bcum[L - 1:L, :]
    kk = (k * jnp.exp(b_end - bcum)).astype(BF16)
    s_new = s_prev * jnp.exp(b_end) + _dot_tn(v, kk)
    st[...] = s_new

    @pl.when(is_ctx)
    def _():
        so_ref[...] = s_new.T


def _hgrn_dir(y0, lb, s0, *, rev):
    L = UNIT
    per = DEC_SEQ // L
    nb = A_QK // LANE
    d = 1 if rev else 0
    tri, msk = _hgrn_consts(rev)

    def rb(n):
        if not rev:
            return n
        m = n - BATCH
        return jnp.where(n < BATCH, n, BATCH + (m // per) * per + (per - 1 - m % per))

    def s0_idx(h, n):
        b = jnp.clip((n - BATCH) // per, 0, DEC_BATCH - 1)
        return (b * 2 * A_HEADS + d * A_HEADS + h, 0, 0)

    o, so = pl.pallas_call(
        functools.partial(_hgrn_kernel, rev=rev),
        grid=(A_HEADS, T // L),
        in_specs=[pl.BlockSpec((L, LANE), lambda h, n: (rb(n), h)),
                  pl.BlockSpec((L, LANE), lambda h, n: (rb(n), nb + h)),
                  pl.BlockSpec((L, LANE), lambda h, n: (rb(n), (2 + d) * nb + h)),
                  pl.BlockSpec((1, LANE), lambda h, n: (0, h)),
                  pl.BlockSpec((None, A_DK, A_DK), s0_idx),
                  pl.BlockSpec((L, L), lambda h, n: (0, 0)),
                  pl.BlockSpec((len(HGRN_LEVELS), L, L), lambda h, n: (0, 0, 0))],
        out_specs=[pl.BlockSpec((L, LANE), lambda h, n: (rb(n), h)),
                   pl.BlockSpec((None, A_DK, A_DK),
                                lambda h, n: (jnp.minimum(n, BATCH - 1) * A_HEADS + h, 0, 0))],
        out_shape=[jax.ShapeDtypeStruct((T, A_QK), F32),
                   jax.ShapeDtypeStruct((BATCH * A_HEADS, A_DK, A_DK), F32)],
        scratch_shapes=[pltpu.VMEM((A_DK, A_DK), F32)],
        compiler_params=_cparams("arbitrary", "arbitrary"),
        name="hgrn_bwd" if rev else "hgrn_fwd",
    )(y0, y0, y0, lb, s0, tri, msk)
    return o, so.reshape(BATCH, A_HEADS, A_DK, A_DK)


def _hgrn_out_kernel(of_ref, ob_ref, g_ref, gn_ref, o_ref):
    o = _rms(of_ref[...] + ob_ref[...], gn_ref[...])
    o_ref[...] = (o * _silu(g_ref[...])).astype(o_ref.dtype)


def _hgrn_out(o_fw, o_bw, y0, g_out):
    tm = 1024
    nb = A_QK // LANE
    blk = lambda off: pl.BlockSpec((tm, LANE), lambda i, h: (i, off + h))
    return pl.pallas_call(
        _hgrn_out_kernel,
        grid=(T // tm, A_HEADS),
        in_specs=[blk(0), blk(0), blk(4 * nb), pl.BlockSpec((1, LANE), lambda i, h: (0, 0))],
        out_specs=blk(0),
        out_shape=jax.ShapeDtypeStruct((T, A_QK), BF16),
        compiler_params=_cparams("arbitrary", "arbitrary"),
        name="hgrn_out",
    )(o_fw, o_bw, y0, g_out.reshape(1, LANE))


CONV_HALO = 16
CONV_RC, CONV_CC = 32, 256


def _conv_kernel(ap_ref, gp_ref, a_ref, g_ref, an_ref, gn_ref, w_ref, b_ref, lg_ref, lb_ref,
                 o_ref, pad, sh, yb):
    L = UNIT
    SUB = 8
    span = L + CONV_HALO + SUB
    n = pl.program_id(0)
    s = (n - BATCH) % (DEC_SEQ // L)
    lat = n >= BATCH
    has_prev = jnp.logical_and(lat, s > 0).astype(F32)
    has_next = jnp.logical_and(lat, s < DEC_SEQ // L - 1).astype(F32)
    glu = lambda a, g: a[...] * _sigmoid(g[...])
    pad[0:CONV_HALO, :] = glu(ap_ref, gp_ref) * has_prev
    pad[CONV_HALO:CONV_HALO + L, :] = glu(a_ref, g_ref)
    pad[CONV_HALO + L:CONV_HALO + L + CONV_HALO, :] = glu(an_ref, gn_ref) * has_next
    base = CONV_HALO - CONV_K // 2
    for b in range(1, SUB):
        sh[b - 1, 0:span, :] = pad[b:b + span, :]
    for c0 in range(0, B_W, CONV_CC):
        w = w_ref[:, c0:c0 + CONV_CC]
        for r0 in range(0, L, CONV_RC):
            acc = jnp.zeros((CONV_RC, CONV_CC), F32)
            for kk in range(CONV_K):
                a, b = divmod(base + kk, SUB)
                lo = r0 + a * SUB
                if b == 0:
                    win = pad[lo:lo + CONV_RC, c0:c0 + CONV_CC]
                else:
                    win = sh[b - 1, lo:lo + CONV_RC, c0:c0 + CONV_CC]
                acc = acc + w[kk:kk + 1, :] * win
            yb[r0:r0 + CONV_RC, c0:c0 + CONV_CC] = acc
    y = yb[...] + b_ref[...]
    mu = jnp.mean(y, axis=-1, keepdims=True)
    yc = y - mu
    yn = yc * lax.rsqrt(jnp.mean(yc * yc, axis=-1, keepdims=True) + EPS) * lg_ref[...] + lb_ref[...]
    o_ref[...] = _silu(yn).astype(o_ref.dtype)


def _conv(y0, conv_w, conv_b, ln_g, ln_b):
    L = UNIT
    ca = (3 * A_QK + 2 * A_QK) // B_W
    hb = L // CONV_HALO
    nhalo = T // CONV_HALO
    prev = lambda off: pl.BlockSpec((CONV_HALO, B_W), lambda n: (jnp.maximum(n * hb - 1, 0), off))
    cur = lambda off: pl.BlockSpec((L, B_W), lambda n: (n, off))
    nxt = lambda off: pl.BlockSpec((CONV_HALO, B_W), lambda n: (jnp.minimum((n + 1) * hb, nhalo - 1), off))
    vec = pl.BlockSpec((1, B_W), lambda n: (0, 0))
    return pl.pallas_call(
        _conv_kernel,
        grid=(T // L,),
        in_specs=[prev(ca), prev(ca + 1), cur(ca), cur(ca + 1), nxt(ca), nxt(ca + 1),
                  pl.BlockSpec((CONV_K, B_W), lambda n: (0, 0)), vec, vec, vec],
        out_specs=pl.BlockSpec((L, B_W), lambda n: (n, 0)),
        out_shape=jax.ShapeDtypeStruct((T, B_W), BF16),
        scratch_shapes=[pltpu.VMEM((L + 2 * CONV_HALO, B_W), F32),
                        pltpu.VMEM((7, L + CONV_HALO + 8, B_W), F32), pltpu.VMEM((L, B_W), F32)],
        compiler_params=_cparams("arbitrary"),
        name="conv",
    )(y0, y0, y0, y0, y0, y0, conv_w, conv_b.reshape(1, B_W), ln_g.reshape(1, B_W), ln_b.reshape(1, B_W))


def _axial_tables(L, rot_dim):
    rows = L // GRID_W
    row = np.repeat(np.arange(rows), GRID_W).astype(np.float32)
    col = np.tile(np.arange(GRID_W), rows).astype(np.float32)
    quarter = rot_dim // 4
    inv = (ROPE_BASE ** (-np.arange(quarter, dtype=np.float32) / quarter)).astype(np.float32)
    ang = np.concatenate([row[:, None] * inv, col[:, None] * inv], axis=-1)
    return np.cos(ang).astype(np.float32), np.sin(ang).astype(np.float32)


def _rope_tables(L, rot_dim, reps):
    cos, sin = _axial_tables(L, rot_dim)
    c = np.tile(np.concatenate([cos, cos], axis=-1), (1, reps))
    s = np.tile(np.concatenate([-sin, sin], axis=-1), (1, reps))
    return jnp.asarray(c), jnp.asarray(s)


def _rope(x, c, s, rot_dim):
    w = x.shape[-1]
    half = rot_dim // 2
    if rot_dim == w:
        swapped = pltpu.roll(x, half, 1)
    else:
        lane = lax.broadcasted_iota(jnp.int32, x.shape, 1)
        swapped = jnp.where((lane % rot_dim) < half, pltpu.roll(x, w - half, 1), pltpu.roll(x, half, 1))
    return x * c + swapped * s


def _softmax_parts(scores):
    m = scores[0].max(axis=-1, keepdims=True)
    for s in scores[1:]:
        m = jnp.maximum(m, s.max(axis=-1, keepdims=True))
    es = [jnp.exp(s - m) for s in scores]
    den = es[0].sum(axis=-1, keepdims=True)
    for e in es[1:]:
        den = den + e.sum(axis=-1, keepdims=True)
    inv = 1.0 / den
    return [e * inv for e in es]


def _diff_kernel(*refs, two_seg, lam_init):
    it = iter(refs)
    lam_ref, q1_ref, q2_ref = next(it), next(it), next(it)
    if two_seg:
        k1c_ref, k2c_ref, vc_ref = next(it), next(it), next(it)
    k1_ref, k2_ref, v_ref = next(it), next(it), next(it)
    if two_seg:
        cq_ref, sq_ref, ck_ref, sk_ref = next(it), next(it), next(it), next(it)
    gs_ref, o_ref = next(it), next(it)
    scale = C_DH ** -0.5
    lam = lam_ref[0, 0]
    outs = []
    ps = []
    for q_ref, kc_ref, k_ref in ((q1_ref, k1c_ref if two_seg else None, k1_ref),
                                 (q2_ref, k2c_ref if two_seg else None, k2_ref)):
        q = q_ref[...]
        k = k_ref[...]
        if two_seg:
            sc = [_dot_nt(q.astype(BF16), kc_ref[...].astype(BF16)) * scale,
                  _dot_nt(_rope(q, cq_ref[...], sq_ref[...], C_DH).astype(BF16),
                          _rope(k, ck_ref[...], sk_ref[...], C_DH).astype(BF16)) * scale]
        else:
            sc = [_dot_nt(q.astype(BF16), k.astype(BF16)) * scale]
        ps.append(_softmax_parts(sc))
    vs = ([vc_ref[...]] if two_seg else []) + [v_ref[...]]
    o = None
    for p1, p2, vv in zip(ps[0], ps[1], vs):
        t = _dot((p1 - lam * p2).astype(BF16), vv.astype(BF16))
        o = t if o is None else o + t
    o_ref[...] = (_rms(o, gs_ref[...]) * (1.0 - lam_init)).astype(o_ref.dtype)


def _diff_attn(y1, lam, g_sub, lam_init, cache_k=None, cache_v=None):
    two_seg = cache_k is not None
    hw = 2 * C_DH
    smem = pl.BlockSpec(memory_space=pltpu.SMEM)
    if not two_seg:
        grid = (BATCH, C_HEADS)
        blk = lambda off, w: pl.BlockSpec((SEQ, w), lambda b, h: (b, off + h * (hw // w)))
        in_specs = [smem, blk(0, C_DH), blk(1, C_DH),
                    blk(C_W // C_DH, C_DH), blk(C_W // C_DH + 1, C_DH), blk(2 * C_W // hw, hw),
                    pl.BlockSpec((1, hw), lambda b, h: (0, 0))]
        args = [lam, y1, y1, y1, y1, y1, g_sub.reshape(1, hw)]
        out_specs = pl.BlockSpec((SEQ, hw), lambda b, h: (b, h))
        out_rows = T_CTX
        sem = ("arbitrary", "arbitrary")
    else:
        tq = 256
        nq = DEC_SEQ // tq
        r0 = T_CTX // DEC_SEQ
        grid = (DEC_BATCH, C_HEADS, nq)
        qblk = lambda off: pl.BlockSpec((tq, C_DH), lambda b, h, i: (T_CTX // tq + b * nq + i, off + 2 * h))
        kblk = lambda off, w: pl.BlockSpec((DEC_SEQ, w), lambda b, h, i: (r0 + b, off + h * (hw // w)))
        cblk = lambda off, w: pl.BlockSpec((PAST, w), lambda b, h, i: (b, off + h * (hw // w)))
        tq_blk = pl.BlockSpec((tq, C_DH), lambda b, h, i: (i, 0))
        tk_blk = pl.BlockSpec((DEC_SEQ, C_DH), lambda b, h, i: (0, 0))
        c, s = _rope_tables(DEC_SEQ, C_DH, 1)
        in_specs = [smem, qblk(0), qblk(1), cblk(0, C_DH), cblk(1, C_DH), cblk(0, hw),
                    kblk(C_W // C_DH, C_DH), kblk(C_W // C_DH + 1, C_DH), kblk(2 * C_W // hw, hw),
                    tq_blk, tq_blk, tk_blk, tk_blk,
                    pl.BlockSpec((1, hw), lambda b, h, i: (0, 0))]
        args = [lam, y1, y1, cache_k, cache_k, cache_v, y1, y1, y1, c, s, c, s, g_sub.reshape(1, hw)]
        out_specs = pl.BlockSpec((tq, hw), lambda b, h, i: (b * nq + i, h))
        out_rows = T_LAT
        sem = ("arbitrary", "arbitrary", "arbitrary")
    return pl.pallas_call(
        functools.partial(_diff_kernel, two_seg=two_seg, lam_init=lam_init),
        grid=grid, in_specs=in_specs, out_specs=out_specs,
        out_shape=jax.ShapeDtypeStruct((out_rows, C_W), BF16),
        compiler_params=_cparams(*sem),
        name="diff_lat" if two_seg else "diff_ctx",
    )(*args)


def _mla_kernel(*refs, two_seg):
    it = iter(refs)
    qn_ref, qr_ref = next(it), next(it)
    if two_seg:
        kvc_ref, krc_ref = next(it), next(it)
    kv_ref, kr_ref = next(it), next(it)
    if two_seg:
        cq_ref, sq_ref, ck_ref, sk_ref = next(it), next(it), next(it), next(it)
    o_ref = next(it)
    scale = (NOPE + ROPE) ** -0.5
    qr = qr_ref[...]
    kr = kr_ref[...]
    if two_seg:
        qr_rot = _rope(qr, cq_ref[...], sq_ref[...], ROPE)
        kr_rot = _rope(kr, ck_ref[...], sk_ref[...], ROPE)
        krc = krc_ref[...]
    lane = lax.broadcasted_iota(jnp.int32, (1, LANE), 1)
    for h in range(D_HEADS):
        half = (lane < ROPE) if h % 2 == 0 else (lane >= ROPE)
        pr = slice((h // 2) * LANE, (h // 2 + 1) * LANE)
        qn = qn_ref[:, h * NOPE:(h + 1) * NOPE]
        kn = kv_ref[:, h * 2 * NOPE:h * 2 * NOPE + NOPE]
        vv = kv_ref[:, h * 2 * NOPE + NOPE:(h + 1) * 2 * NOPE]
        cat = lambda a, b: jnp.concatenate([a.astype(BF16), b.astype(BF16)], axis=-1)
        if two_seg:
            knc = kvc_ref[:, h * 2 * NOPE:h * 2 * NOPE + NOPE]
            vc = kvc_ref[:, h * 2 * NOPE + NOPE:(h + 1) * 2 * NOPE]
            sc = [_dot_nt(cat(qn, qr[:, pr]), cat(knc, jnp.where(half, krc, 0.0))) * scale,
                  _dot_nt(cat(qn, qr_rot[:, pr]), cat(kn, jnp.where(half, kr_rot, 0.0))) * scale]
            p = _softmax_parts(sc)
            o = _dot(p[0].astype(BF16), vc) + _dot(p[1].astype(BF16), vv)
        else:
            sc = [_dot_nt(cat(qn, qr[:, pr]), cat(kn, jnp.where(half, kr, 0.0))) * scale]
            o = _dot(_softmax_parts(sc)[0].astype(BF16), vv)
        o_ref[:, h * V_DIM:(h + 1) * V_DIM] = o.astype(o_ref.dtype)


def _mla_attn(qm, kv, y1, kv_c=None, kr_c=None):
    two_seg = kv_c is not None
    qn_w, qr_w = D_HEADS * NOPE, D_HEADS * ROPE
    kr_col = (Q_RANK + KV_RANK) // LANE
    if not two_seg:
        grid = (BATCH,)
        in_specs = [pl.BlockSpec((SEQ, qn_w), lambda b: (b, 0)),
                    pl.BlockSpec((SEQ, qr_w), lambda b: (b, qn_w // qr_w)),
                    pl.BlockSpec((SEQ, 2 * qn_w), lambda b: (b, 0)),
                    pl.BlockSpec((SEQ, LANE), lambda b: (b, kr_col))]
        args = [qm, qm, kv, y1]
        out_specs = pl.BlockSpec((SEQ, qn_w), lambda b: (b, 0))
        out_rows = T_CTX
        sem = ("arbitrary",)
    else:
        tq = 256
        nq = DEC_SEQ // tq
        r0 = T_CTX // DEC_SEQ
        grid = (DEC_BATCH, nq)
        cq, sq = _rope_tables(DEC_SEQ, ROPE, qr_w // ROPE)
        ck, sk = _rope_tables(DEC_SEQ, ROPE, LANE // ROPE)
        in_specs = [pl.BlockSpec((tq, qn_w), lambda b, i: (T_CTX // tq + b * nq + i, 0)),
                    pl.BlockSpec((tq, qr_w), lambda b, i: (T_CTX // tq + b * nq + i, qn_w // qr_w)),
                    pl.BlockSpec((PAST, 2 * qn_w), lambda b, i: (b, 0)),
                    pl.BlockSpec((PAST, LANE), lambda b, i: (b, 0)),
                    pl.BlockSpec((DEC_SEQ, 2 * qn_w), lambda b, i: (r0 + b, 0)),
                    pl.BlockSpec((DEC_SEQ, LANE), lambda b, i: (r0 + b, kr_col)),
                    pl.BlockSpec((tq, qr_w), lambda b, i: (i, 0)),
                    pl.BlockSpec((tq, qr_w), lambda b, i: (i, 0)),
                    pl.BlockSpec((DEC_SEQ, LANE), lambda b, i: (0, 0)),
                    pl.BlockSpec((DEC_SEQ, LANE), lambda b, i: (0, 0))]
        args = [qm, qm, kv_c, kr_c, kv, y1, cq, sq, ck, sk]
        out_specs = pl.BlockSpec((tq, qn_w), lambda b, i: (b * nq + i, 0))
        out_rows = T_LAT
        sem = ("arbitrary", "arbitrary")
    return pl.pallas_call(
        functools.partial(_mla_kernel, two_seg=two_seg),
        grid=grid, in_specs=in_specs, out_specs=out_specs,
        out_shape=jax.ShapeDtypeStruct((out_rows, qn_w), BF16),
        compiler_params=_cparams(*sem),
        name="mla_lat" if two_seg else "mla_ctx",
    )(*args)


def _route_kernel(lg_ref, tril_ref, o_ref, cnt_ref, carry):
    tb = ROUTE_TB

    @pl.when(pl.program_id(0) == 0)
    def _():
        carry[...] = jnp.zeros_like(carry)

    l = lg_ref[...]
    lane = lax.broadcasted_iota(jnp.int32, (tb, N_EXPERTS), 1).astype(F32)
    vals, ids, sels = [], [], []
    for _ in range(TOP_K):
        m = l.max(axis=-1, keepdims=True)
        idx = jnp.min(jnp.where(l == m, lane, float(N_EXPERTS)), axis=-1, keepdims=True)
        sel = lane == idx
        vals.append(m)
        ids.append(idx)
        sels.append(sel)
        l = jnp.where(sel, -jnp.inf, l)
    es = [jnp.exp(v - vals[0]) for v in vals]
    den = es[0]
    for e in es[1:]:
        den = den + e
    inv = 1.0 / den
    picked = jnp.zeros((tb, N_EXPERTS), F32)
    for sel in sels:
        picked = picked + jnp.where(sel, 1.0, 0.0)
    base = carry[...] + _dot(tril_ref[...], picked.astype(BF16))
    carry[...] = carry[...] + jnp.sum(picked, axis=0, keepdims=True)
    cnt_ref[...] = carry[...]
    out_lane = lax.broadcasted_iota(jnp.int32, (tb, LANE), 1)
    out = jnp.zeros((tb, LANE), F32)
    for k in range(TOP_K):
        rank = jnp.sum(jnp.where(sels[k], base, 0.0), axis=-1, keepdims=True)
        out = jnp.where(out_lane == k, ids[k], out)
        out = jnp.where(out_lane == TOP_K + k, es[k] * inv, out)
        out = jnp.where(out_lane == 2 * TOP_K + k, rank, out)
    o_ref[...] = out


def _route(logits):
    tb = ROUTE_TB
    tril = jnp.asarray(np.tril(np.ones((tb, tb), np.float32), -1), BF16)
    packed, counts = pl.pallas_call(
        _route_kernel,
        grid=(T // tb,),
        in_specs=[pl.BlockSpec((tb, N_EXPERTS), lambda i: (i, 0)),
                  pl.BlockSpec((tb, tb), lambda i: (0, 0))],
        out_specs=[pl.BlockSpec((tb, LANE), lambda i: (i, 0)),
                   pl.BlockSpec((1, N_EXPERTS), lambda i: (0, 0))],
        out_shape=[jax.ShapeDtypeStruct((T, LANE), F32), jax.ShapeDtypeStruct((1, N_EXPERTS), F32)],
        scratch_shapes=[pltpu.VMEM((1, N_EXPERTS), F32)],
        compiler_params=_cparams("arbitrary"),
        name="moe_route",
    )(logits, tril)
    eid = packed[:, 0:TOP_K].astype(jnp.int32)
    gates = packed[:, TOP_K:2 * TOP_K]
    rank = packed[:, 2 * TOP_K:3 * TOP_K].astype(jnp.int32)
    return eid, gates, rank, counts[0].astype(jnp.int32)


def _smem_rows(a, tb):
    return a.reshape(T // tb, tb, TOP_K).transpose(0, 2, 1).reshape(T // tb, 1, TOP_K * tb)


def _dispatch_kernel(pstart_ref, padst_ref, padn_ref, nu_ref, eid_ref, rank_ref, h_ref, xs_ref, dest_ref,
                     zblk, sem, zsem):
    tb = DISPATCH_TB
    i = pl.program_id(0)

    def body(t, c):
        for k in range(TOP_K):
            d = pstart_ref[eid_ref[0, 0, k * tb + t]] + rank_ref[0, 0, k * tb + t]
            dest_ref[0, 0, k * tb + t] = d
            pltpu.make_async_copy(h_ref.at[t], xs_ref.at[d], sem).start()
        return c
    lax.fori_loop(0, tb, body, 0)

    def pad_rows(act):
        def per_e(e, c):
            n, d = padn_ref[e], padst_ref[e]
            p = MOE_TM // 2
            while p:
                @pl.when((n & p) != 0)
                def _(d=d, p=p):
                    act(pltpu.make_async_copy(zblk.at[pl.ds(0, p)], xs_ref.at[pl.ds(d, p)], zsem))
                d = d + (n & p)
                p //= 2
            return c
        lax.fori_loop(0, N_EXPERTS, per_e, 0)

    def pad_blocks(act):
        def per_b(b, c):
            act(pltpu.make_async_copy(zblk, xs_ref.at[pl.ds(pl.multiple_of(b * MOE_TM, MOE_TM), MOE_TM)], zsem))
            return c
        lax.fori_loop(nu_ref[0], MOE_BLOCKS, per_b, 0)

    @pl.when(i == 0)
    def _():
        zblk[...] = jnp.zeros_like(zblk)
        pad_rows(lambda cp: cp.start())
        pad_blocks(lambda cp: cp.start())
        pad_rows(lambda cp: cp.wait())
        pad_blocks(lambda cp: cp.wait())

    for k in range(TOP_K):
        pltpu.make_async_copy(h_ref, xs_ref.at[pl.ds(0, tb)], sem).wait()


def _dispatch(h3, eid, rank, counts):
    tb = DISPATCH_TB
    padded = (counts + MOE_TM - 1) // MOE_TM * MOE_TM
    pend = jnp.cumsum(padded)
    pstart = pend - padded
    n_used = (pend[-1:] // MOE_TM).astype(jnp.int32)
    blk_start = jnp.arange(MOE_BLOCKS, dtype=jnp.int32) * MOE_TM
    blk_expert = jnp.minimum(jnp.sum((pend[None, :] <= blk_start[:, None]).astype(jnp.int32), axis=1),
                             N_EXPERTS - 1).astype(jnp.int32)
    nxt_first = (pend // MOE_TM)[blk_expert].astype(jnp.int32)
    smem = lambda: pl.BlockSpec((1, 1, TOP_K * tb), lambda i, *_: (i, 0, 0), memory_space=pltpu.SMEM)
    xs, dest = pl.pallas_call(
        _dispatch_kernel,
        grid_spec=pltpu.PrefetchScalarGridSpec(
            num_scalar_prefetch=4, grid=(T // tb,),
            in_specs=[smem(), smem(), pl.BlockSpec((tb, PACK_ROWS, LANE), lambda i, *_: (i, 0, 0))],
            out_specs=[pl.BlockSpec(memory_space=pl.ANY), smem()],
            scratch_shapes=[pltpu.VMEM((MOE_TM, PACK_ROWS, LANE), F32),
                            pltpu.SemaphoreType.DMA, pltpu.SemaphoreType.DMA]),
        out_shape=[jax.ShapeDtypeStruct((MOE_ROWS, PACK_ROWS, LANE), F32),
                   jax.ShapeDtypeStruct((T // tb, 1, TOP_K * tb), jnp.int32)],
        compiler_params=_cparams("arbitrary"),
        name="moe_dispatch",
    )(pstart.astype(jnp.int32), (pstart + counts).astype(jnp.int32), (padded - counts).astype(jnp.int32),
      n_used, _smem_rows(eid, tb), _smem_rows(rank, tb), h3)
    return xs, dest, (blk_expert, n_used, nxt_first)


def _swiglu(gate, up):
    gate = jnp.minimum(gate, SWIGLU_LIMIT)
    up = jnp.clip(up, -SWIGLU_LIMIT, SWIGLU_LIMIT)
    return gate * _sigmoid(SWIGLU_ALPHA * gate) * (up + 1.0)


def _new_expert(be_ref, i):
    return jnp.logical_or(i == 0, be_ref[i] != be_ref[jnp.maximum(i - 1, 0)])


def _stream_weights(be_ref, nu_ref, nxt_ref, w_hbms, w_f32s, w_bfs, sems, *, l, tn):
    j, i = pl.program_id(0), pl.program_id(1)

    def copies(e, jj):
        col = pl.multiple_of(jj * tn, tn)
        return [pltpu.make_async_copy(w.at[l, e, :, pl.ds(col, tn)], buf, sems.at[k])
                for k, (w, buf) in enumerate(zip(w_hbms, w_f32s))]

    @pl.when(_new_expert(be_ref, i))
    def _():
        @pl.when(jnp.logical_and(i == 0, j == 0))
        def _():
            for cp in copies(be_ref[0], 0):
                cp.start()

        for cp in copies(be_ref[i], j):
            cp.wait()
        for buf, bf in zip(w_f32s, w_bfs):
            bf[...] = buf[...].astype(BF16)
        nf = nxt_ref[i]
        more = nf < nu_ref[0]

        @pl.when(more)
        def _():
            for cp in copies(be_ref[jnp.minimum(nf, MOE_BLOCKS - 1)], j):
                cp.start()

        @pl.when(jnp.logical_and(jnp.logical_not(more), j + 1 < pl.num_programs(0)))
        def _():
            for cp in copies(be_ref[0], j + 1):
                cp.start()


def _moe_up_kernel(be_ref, nu_ref, nxt_ref, x_ref, wg_hbm, wu_hbm, bg_ref, bu_ref, o_ref,
                   wg_f32, wu_f32, wg_bf, wu_bf, x_bf, sems, *, l, tn):
    i = pl.program_id(1)

    @pl.when(i < nu_ref[0])
    def _():
        _stream_weights(be_ref, nu_ref, nxt_ref, (wg_hbm, wu_hbm), (wg_f32, wu_f32), (wg_bf, wu_bf), sems,
                        l=l, tn=tn)
        _load_packed_rows(x_ref, x_bf, x_bf.shape[0])
        x = x_bf[...]
        gate = _dot(x, wg_bf[...]) + bg_ref[...]
        up = _dot(x, wu_bf[...]) + bu_ref[...]
        o_ref[...] = _swiglu(gate, up).astype(o_ref.dtype)

    @pl.when(i >= nu_ref[0])
    def _():
        o_ref[...] = jnp.zeros_like(o_ref)


def _moe_down_kernel(be_ref, nu_ref, nxt_ref, h_ref, wd_hbm, bd_ref, o_ref, wd_f32, wd_bf, sems, *, l, tn):
    i = pl.program_id(1)

    @pl.when(i < nu_ref[0])
    def _():
        _stream_weights(be_ref, nu_ref, nxt_ref, (wd_hbm,), (wd_f32,), (wd_bf,), sems, l=l, tn=tn)
        o_ref[...] = _dot(h_ref[...], wd_bf[...]) + bd_ref[...]

    @pl.when(i >= nu_ref[0])
    def _():
        o_ref[...] = jnp.zeros_like(o_ref)


def _moe_experts(xs, tables, l, w_gate, b_gate, w_up, b_up, w_down, b_down):
    tm = MOE_TM
    row = lambda j, i, be, nu, nx: jnp.minimum(i, nu[0] - 1)
    exp = lambda j, i, be, nu, nx: be[jnp.minimum(i, nu[0] - 1)]
    x3spec = pl.BlockSpec((tm * PACK_ROWS, LANE), lambda *a: (row(*a), 0))
    xspec = pl.BlockSpec((tm, D), lambda *a: (row(*a), 0))
    bspec = lambda tn: pl.BlockSpec((None, None, 1, tn), lambda j, i, be, nu, nx: (l, exp(j, i, be, nu, nx), 0, j))
    ospec = lambda tn: pl.BlockSpec((tm, tn), lambda j, i, be, nu, nx: (i, j))
    hbm = pl.BlockSpec(memory_space=pl.ANY)
    b4 = lambda b: b.reshape(DEPTH, N_EXPERTS, 1, -1)
    tn = MOE_TN_UP
    hmid = pl.pallas_call(
        functools.partial(_moe_up_kernel, l=l, tn=tn),
        grid_spec=pltpu.PrefetchScalarGridSpec(
            num_scalar_prefetch=3, grid=(D_FF // tn, MOE_BLOCKS),
            in_specs=[x3spec, hbm, hbm, bspec(tn), bspec(tn)],
            out_specs=ospec(tn),
            scratch_shapes=[pltpu.VMEM((D, tn), F32), pltpu.VMEM((D, tn), F32),
                            pltpu.VMEM((D, tn), BF16), pltpu.VMEM((D, tn), BF16),
                            pltpu.VMEM((tm, D), BF16), pltpu.SemaphoreType.DMA((2,))]),
        out_shape=jax.ShapeDtypeStruct((MOE_ROWS, D_FF), BF16),
        compiler_params=_cparams("arbitrary", "arbitrary"),
        name="moe_up",
    )(*tables, xs, w_gate, w_up, b4(b_gate), b4(b_up))
    tn = MOE_TN_DOWN
    return pl.pallas_call(
        functools.partial(_moe_down_kernel, l=l, tn=tn),
        grid_spec=pltpu.PrefetchScalarGridSpec(
            num_scalar_prefetch=3, grid=(D // tn, MOE_BLOCKS),
            in_specs=[xspec, hbm, bspec(tn)],
            out_specs=ospec(tn),
            scratch_shapes=[pltpu.VMEM((D_FF, tn), F32), pltpu.VMEM((D_FF, tn), BF16),
                            pltpu.SemaphoreType.DMA((1,))]),
        out_shape=jax.ShapeDtypeStruct((MOE_ROWS, D), F32),
        compiler_params=_cparams("arbitrary", "arbitrary"),
        name="moe_down",
    )(*tables, hmid, w_down, b4(b_down))


def _combine_kernel(dest_ref, destn_ref, y_ref, gates_ref, x_ref, mods_ref, g_ref, *rest,
                    tb, nblk, mode, mod_row, emit_x):
    if emit_x:
        x2_ref, h_ref, buf, sem = rest
    else:
        hc_ref, hl_ref, buf, sem = rest
    i = pl.program_id(0)
    per = DISPATCH_TB // tb

    def issue(dref, step, slot):
        off = (step % per) * tb

        def body(r, c):
            for k in range(TOP_K):
                d = dref[0, 0, k * DISPATCH_TB + off + r]
                pltpu.make_async_copy(y_ref.at[pl.ds(d, 1), :],
                                      buf.at[slot, k, pl.ds(r, 1), :], sem.at[slot]).start()
            return c
        lax.fori_loop(0, tb, body, 0)

    @pl.when(i == 0)
    def _():
        issue(dest_ref, i, 0)

    @pl.when(i + 1 < nblk)
    def _():
        issue(destn_ref, i + 1, (i + 1) % 2)

    slot = i % 2
    for k in range(TOP_K):
        pltpu.make_async_copy(y_ref.at[pl.ds(0, tb), :], buf.at[slot, k], sem.at[slot]).wait()
    gates = gates_ref[...]
    moe = gates[:, 0:1] * buf[slot, 0]
    for k in range(1, TOP_K):
        moe = moe + gates[:, k:k + 1] * buf[slot, k]
    x, h = _resid_norm(x_ref[...], moe, mods_ref[...], g_ref[...],
                       has_resid=True, gate_row=5, mode=mode, mod_row=mod_row)
    if emit_x:
        x2_ref[...] = x
        h_ref[...] = h.astype(h_ref.dtype)
    else:
        @pl.when(i < T_CTX // tb)
        def _():
            hc_ref[...] = h

        @pl.when(i >= T_CTX // tb)
        def _():
            hl_ref[...] = h


def _moe_combine(y, route, x, mods_l, g, *, mode, mods_next=None):
    tb = COMBINE_TB
    nblk = T // tb
    per = DISPATCH_TB // tb
    dest, gates = route
    emit_x = mode == "mod"
    if emit_x:
        mods = jnp.concatenate([mods_next[:, 0:2], mods_l[:, 2:]], axis=1)
    else:
        mods = mods_l
    row = pl.BlockSpec((tb, D), lambda i: (i, 0))
    nctx = T_CTX // tb
    table = lambda step: pl.BlockSpec((1, 1, TOP_K * DISPATCH_TB), lambda i: (step(i) // per, 0, 0),
                                      memory_space=pltpu.SMEM)
    if emit_x:
        out_specs = [row, row]
        out_shape = [jax.ShapeDtypeStruct((T, D), F32), jax.ShapeDtypeStruct((T, D), BF16)]
    else:
        out_specs = [pl.BlockSpec((tb, D), lambda i: (jnp.minimum(i, nctx - 1), 0)),
                     pl.BlockSpec((tb, D), lambda i: (jnp.maximum(i - nctx, 0), 0))]
        out_shape = [jax.ShapeDtypeStruct((T_CTX, D), F32), jax.ShapeDtypeStruct((T_LAT, D), F32)]
    return pl.pallas_call(
        functools.partial(_combine_kernel, tb=tb, nblk=nblk, mode=mode, mod_row=0, emit_x=emit_x),
        grid=(nblk,),
        in_specs=[table(lambda i: i), table(lambda i: jnp.minimum(i + 1, nblk - 1)),
                  pl.BlockSpec(memory_space=pl.ANY),
                  pl.BlockSpec((tb, TOP_K), lambda i: (i, 0)),
                  row,
                  pl.BlockSpec((None, 8, D), lambda i: (_seg_of_block(i, tb), 0, 0)),
                  pl.BlockSpec((1, D), lambda i: (0, 0))],
        out_specs=out_specs, out_shape=out_shape,
        scratch_shapes=[pltpu.VMEM((2, TOP_K, tb, D), F32), pltpu.SemaphoreType.DMA((2,))],
        compiler_params=_cparams("arbitrary"),
        name="moe_combine",
    )(dest, dest, y, gates, x, mods, g.reshape(1, D))


def _moe(h3, logits, l, w_gate, b_gate, w_up, b_up, w_down, b_down):
    eid, gates, rank, counts = _route(logits)
    xs, dest, tables = _dispatch(h3.reshape(T, PACK_ROWS, LANE), eid, rank, counts)
    y = _moe_experts(xs.reshape(MOE_ROWS * PACK_ROWS, LANE), tables, l,
                     w_gate, b_gate, w_up, b_up, w_down, b_down)
    return y, (dest, gates)


def kernel(x_prompt, x_sample, state_hgrn, cache_diff_k, cache_diff_v, cache_mla_ckv, cache_mla_kr, c, c_ctx,
           norm_mix, norm_ffn, w_ada, b_ada, hgrn_lb, w_in_even, g_hgrn_out, conv_w, conv_b, conv_ln_g, conv_ln_b,
           w_out_even, w_in_odd, w_uq, w_ukv, g_q, g_kv, diff_lambda, g_sub, w_out_odd,
           w_router, b_router, w_gate, b_gate, w_up, b_up, w_down, b_down, norm_final):
    x = (x_prompt.reshape(T_CTX, D), x_sample.reshape(T_LAT, D))
    cvec = jnp.concatenate([c_ctx[None, :], c, jnp.zeros((8 - 1 - DEC_BATCH, D), F32)], axis=0)
    ada = _ada(cvec, w_ada, b_ada)
    mods = [jnp.pad(ada[l, :1 + DEC_BATCH].reshape(1 + DEC_BATCH, 6, D), ((0, 0), (0, 2), (0, 0)))
            for l in range(DEPTH)]
    lower_bounds = jnp.cumsum(jax.nn.softmax(hgrn_lb.astype(F32), axis=0), axis=0)
    moe_w = lambda: (w_gate, b_gate, w_up, b_up, w_down, b_down)

    (h,) = _rmod(x, None, mods[0], norm_mix[0], mode="mod", mod_row=0)
    y0 = _mm(h, w_in_even[0], name="in_even")
    s0 = state_hgrn[:, 0].reshape(DEC_BATCH * 2 * A_HEADS, A_DK, A_DK)
    lb = lower_bounds[0].reshape(1, A_QK)
    o_fw, s_fw = _hgrn_dir(y0, lb, s0, rev=False)
    o_bw, s_bw = _hgrn_dir(y0, lb, s0, rev=True)
    o_a = _hgrn_out(o_fw, o_bw, y0, g_hgrn_out[0])
    o_b = _conv(y0, conv_w[0], conv_b[0], conv_ln_g[0], conv_ln_b[0])
    o = _mm([o_a, o_b], w_out_even[0], name="out_even")
    x, h3, logits = _rmod(x, o, mods[0], norm_ffn[0], gate_row=2, mode="mod", mod_row=3, rows3d=True,
                          router=(w_router, b_router, 0))
    y, route = _moe(h3, logits, 0, *moe_w())
    x, h = _moe_combine(y, route, x, mods[0], norm_mix[1], mode="mod", mods_next=mods[1])
    new_hgrn = jnp.stack([s_fw, s_bw], axis=1)[:, None]

    y1 = _mm(h, w_in_odd[0], n_cols=3 * C_W, name="in_odd")
    w_tail = jnp.concatenate([w_in_odd[0][:, 3 * C_W:], w_in_odd[0][:, ODD_IN - ROPE:]], axis=1)
    y1t = _mm(h, w_tail, name="in_odd_tail")
    perm = np.concatenate([np.arange(D_HEADS)[:, None] * (NOPE + ROPE) + np.arange(NOPE)[None, :],
                           np.arange(D_HEADS)[:, None] * (NOPE + ROPE) + NOPE + np.arange(ROPE)[None, :]],
                          axis=None)
    qm = _mm((y1t, Q_RANK, 0), w_uq[0][:, perm], g=g_q[0], tm=1024, tn=D_HEADS * (NOPE + ROPE), name="uq")
    kv, ckv = _mm((y1t, KV_RANK, Q_RANK // KV_RANK), w_ukv[0], g=g_kv[0], emit_norm=True, out_dtype=BF16,
                  tm=1024, tn=2048, name="ukv")
    kv_c = _mm(cache_mla_ckv[:, 0].reshape(DEC_BATCH * PAST, KV_RANK), w_ukv[0], out_dtype=BF16,
               tm=1024, tn=2048, name="ukv_cache")
    kr_c = cache_mla_kr[:, 0].reshape(DEC_BATCH * PAST, ROPE)
    kr_c = jnp.concatenate([kr_c, kr_c], axis=-1)
    lq = diff_lambda[0].astype(F32)
    lam_init = 0.8 - 0.6 * math.exp(-0.3 * 1)
    lam = (jnp.exp(jnp.sum(lq[0] * lq[1])) - jnp.exp(jnp.sum(lq[2] * lq[3])) + lam_init).reshape(1, 1)
    ck = cache_diff_k[:, 0].reshape(DEC_BATCH * PAST, C_W)
    cv = cache_diff_v[:, 0].reshape(DEC_BATCH * PAST, C_W)
    o_c = jnp.concatenate([_diff_attn(y1, lam, g_sub[0], lam_init),
                           _diff_attn(y1, lam, g_sub[0], lam_init, ck, cv)], axis=0)
    o_d = jnp.concatenate([_mla_attn(qm, kv, y1t), _mla_attn(qm, kv, y1t, kv_c, kr_c)], axis=0)
    o = _mm([o_c, o_d], w_out_odd[0], name="out_odd")
    x, h3, logits = _rmod(x, o, mods[1], norm_ffn[1], gate_row=2, mode="mod", mod_row=3, rows3d=True,
                          router=(w_router, b_router, 1))
    y, route = _moe(h3, logits, 1, *moe_w())
    y_ctx, y_lat = _moe_combine(y, route, x, mods[1], norm_final, mode="final")

    ctx = lambda a, shape: a[:T_CTX].reshape(shape)
    kr0 = Q_RANK + KV_RANK
    return (y_ctx.reshape(BATCH, SEQ, D), y_lat.reshape(DEC_BATCH, DEC_SEQ, D),
            new_hgrn,
            ctx(y1[:, C_W:2 * C_W], (BATCH, 1, SEQ, C_HEADS, 2 * C_DH)),
            ctx(y1[:, 2 * C_W:3 * C_W], (BATCH, 1, SEQ, C_HEADS, 2 * C_DH)),
            ctx(ckv, (BATCH, 1, SEQ, KV_RANK)),
            ctx(y1t[:, kr0:kr0 + ROPE], (BATCH, 1, SEQ, ROPE)))
```

```python
import functools
import math

import numpy as np
import jax
import jax.numpy as jnp
from jax import lax
from jax.experimental import pallas as pl
from jax.experimental.pallas import tpu as pltpu

F32 = jnp.float32
BF16 = jnp.bfloat16

D = 2048
BATCH, SEQ = 32, 256
DEC_BATCH, DEC_SEQ = 2, 1024
PAST = 512
DEPTH = 2
GRID_W = 64
T_CTX = BATCH * SEQ
T_LAT = DEC_BATCH * DEC_SEQ
T = T_CTX + T_LAT
A_HEADS, A_DK = 8, 128
A_QK = A_HEADS * A_DK
B_W = 1024
CONV_K = 31
C_HEADS, C_DH = 4, 128
C_W = C_HEADS * 2 * C_DH
D_HEADS = 8
Q_RANK, KV_RANK = 512, 256
NOPE, ROPE, V_DIM = 128, 64, 128
ROPE_BASE = 10000.0
N_EXPERTS, TOP_K = 32, 4
D_FF = 2048
SWIGLU_ALPHA, SWIGLU_LIMIT = 1.702, 7.0
EVEN_IN = 3 * A_QK + 2 * A_QK + 2 * B_W
ODD_IN = 3 * C_W + Q_RANK + KV_RANK + ROPE
EPS = 1e-6

LANE = 128
UNIT = 256
ROW_TILE = 256
MM_TM, MM_TN = 1024, 1024
MOE_TM = 512
MOE_TN_UP = 1024
MOE_TN_DOWN = 2048
MOE_ROWS = T * TOP_K + N_EXPERTS * MOE_TM
MOE_BLOCKS = MOE_ROWS // MOE_TM
ROUTE_TB = 256
DISPATCH_TB = 256
COMBINE_TB = 128
VMEM_LIMIT = 60 * 1024 * 1024
HGRN_LEVELS = (16, 32, 64, 128, 256)
HGRN_CLAMP = 40.0


def _cparams(*sem):
    return pltpu.CompilerParams(dimension_semantics=sem, vmem_limit_bytes=VMEM_LIMIT)


def _dot(a, b):
    return lax.dot_general(a, b, (((1,), (0,)), ((), ())), preferred_element_type=F32)


def _dot_nt(a, b):
    return lax.dot_general(a, b, (((1,), (1,)), ((), ())), preferred_element_type=F32)


def _dot_tn(a, b):
    return lax.dot_general(a, b, (((0,), (0,)), ((), ())), preferred_element_type=F32)


def _sigmoid(x):
    return 1.0 / (1.0 + jnp.exp(-x))


def _silu(x):
    return x * _sigmoid(x)


def _rms(x, g):
    return x * lax.rsqrt(jnp.mean(x * x, axis=-1, keepdims=True) + EPS) * g


def _seg_of_block(i, rows):
    nctx = T_CTX // rows
    return jnp.where(i < nctx, 0, 1 + (i - nctx) // (DEC_SEQ // rows))


def _ada_kernel(c_ref, w_ref, b_ref, o_ref):
    a = _silu(c_ref[...]).astype(BF16)
    o_ref[...] = _dot(a, w_ref[...].astype(BF16)) + b_ref[...]


def _ada(cvec, w_ada, b_ada):
    tn = 1024
    return pl.pallas_call(
        _ada_kernel,
        grid=(DEPTH, 6 * D // tn),
        in_specs=[pl.BlockSpec((8, D), lambda l, j: (0, 0)),
                  pl.BlockSpec((None, D, tn), lambda l, j: (l, 0, j)),
                  pl.BlockSpec((None, 1, tn), lambda l, j: (l, 0, j))],
        out_specs=pl.BlockSpec((None, 8, tn), lambda l, j: (l, 0, j)),
        out_shape=jax.ShapeDtypeStruct((DEPTH, 8, 6 * D), F32),
        compiler_params=_cparams("arbitrary", "arbitrary"),
        name="ada",
    )(cvec, w_ada, b_ada.reshape(DEPTH, 1, 6 * D))


def _mm_kernel(*refs, n_x, rms, emit_norm):
    it = iter(refs)
    x_refs = [next(it) for _ in range(n_x)]
    w_ref = next(it)
    g_ref = next(it) if rms else None
    o_ref = next(it)
    n_ref = next(it) if emit_norm else None
    wbf = next(it)

    @pl.when(pl.program_id(1) == 0)
    def _():
        wbf[...] = w_ref[...].astype(BF16)

    acc = None
    k0 = 0
    for x_ref in x_refs:
        x = x_ref[...]
        if rms:
            x = _rms(x.astype(F32), g_ref[...])
            if emit_norm:
                n_ref[...] = x
        part = _dot(x.astype(BF16), wbf[k0:k0 + x.shape[1], :])
        acc = part if acc is None else acc + part
        k0 += x.shape[1]
    o_ref[...] = acc.astype(o_ref.dtype)


def _mm(xs, w, *, out_dtype=F32, tm=MM_TM, tn=MM_TN, g=None, emit_norm=False, n_cols=None, name="mm"):
    xs = xs if isinstance(xs, list) else [xs]
    xs = [x if isinstance(x, tuple) else (x, x.shape[1], 0) for x in xs]
    M = xs[0][0].shape[0]
    K = sum(wd for _, wd, _ in xs)
    N = w.shape[1] if n_cols is None else n_cols
    assert w.shape[0] == K and (g is None or len(xs) == 1)
    tn = min(tn, N)
    nj = pl.cdiv(N, tn)
    assert M % tm == 0 and (not emit_norm or nj == 1)
    in_specs = [pl.BlockSpec((tm, wd), functools.partial(lambda j, i, cb: (i, cb), cb=cb)) for _, wd, cb in xs]
    in_specs.append(pl.BlockSpec((K, tn), lambda j, i: (0, j)))
    args = [x for x, _, _ in xs] + [w]
    if g is not None:
        in_specs.append(pl.BlockSpec((1, K), lambda j, i: (0, 0)))
        args.append(g.reshape(1, K))
    out_specs = [pl.BlockSpec((tm, tn), lambda j, i: (i, j))]
    out_shape = [jax.ShapeDtypeStruct((M, N), out_dtype)]
    if emit_norm:
        out_specs.append(pl.BlockSpec((tm, K), lambda j, i: (i, 0)))
        out_shape.append(jax.ShapeDtypeStruct((M, K), F32))
    res = pl.pallas_call(
        functools.partial(_mm_kernel, n_x=len(xs), rms=g is not None, emit_norm=emit_norm),
        grid=(nj, M // tm),
        in_specs=in_specs, out_specs=out_specs, out_shape=out_shape,
        scratch_shapes=[pltpu.VMEM((K, tn), BF16)],
        compiler_params=_cparams("arbitrary", "arbitrary"),
        name=name,
    )(*args)
    return res if emit_norm else res[0]


def _resid_norm(x, o, mods, g, *, has_resid, gate_row, mode, mod_row):
    if has_resid:
        x = x + mods[gate_row:gate_row + 1, :] * o
    h = _rms(x, g)
    if mode == "mod":
        h = h * (1.0 + mods[mod_row + 1:mod_row + 2, :]) + mods[mod_row:mod_row + 1, :]
    return x, h


PACK_ROWS = D // LANE


def _store_packed_rows(ref, h):
    for cc in range(PACK_ROWS):
        ref[pl.ds(cc, h.shape[0], stride=PACK_ROWS), :] = h[:, cc * LANE:(cc + 1) * LANE]


def _load_packed_rows(ref, dst, rows):
    for cc in range(PACK_ROWS):
        dst[:, cc * LANE:(cc + 1) * LANE] = ref[pl.ds(cc, rows, stride=PACK_ROWS), :].astype(BF16)


def _rmod_kernel(*refs, has_resid, gate_row, mode, mod_row, emit_logits, rows3d, split_x):
    it = iter(refs)
    x_ref = next(it)
    xb_ref = next(it) if split_x else None
    o_ref = next(it) if has_resid else None
    mods_ref, g_ref = next(it), next(it)
    if emit_logits:
        wr_ref, br_ref = next(it), next(it)
    x1_ref = next(it) if has_resid else None
    h_ref = next(it)
    x = x_ref[...]
    if split_x:
        x = jnp.where(pl.program_id(0) < T_CTX // ROW_TILE, x, xb_ref[...])
    x, h = _resid_norm(x, o_ref[...] if has_resid else None, mods_ref[...], g_ref[...],
                       has_resid=has_resid, gate_row=gate_row, mode=mode, mod_row=mod_row)
    if has_resid:
        x1_ref[...] = x
    if rows3d:
        _store_packed_rows(h_ref, h)
    else:
        h_ref[...] = h.astype(h_ref.dtype)
    if emit_logits:
        lg_ref = next(it)
        lg_ref[...] = lax.dot_general(h, wr_ref[...], (((1,), (0,)), ((), ())),
                                      precision=lax.Precision.HIGHEST,
                                      preferred_element_type=F32) + br_ref[...]


def _rmod(x, o, mods, g, *, gate_row=0, mode="mod", mod_row=0, h_dtype=BF16, router=None, rows3d=False):
    tm = ROW_TILE
    has_resid = o is not None
    split_x = isinstance(x, tuple)
    row = pl.BlockSpec((tm, D), lambda i: (i, 0))
    if split_x:
        nctx = T_CTX // tm
        in_specs = [pl.BlockSpec((tm, D), lambda i: (jnp.minimum(i, nctx - 1), 0)),
                    pl.BlockSpec((tm, D), lambda i: (jnp.maximum(i - nctx, 0), 0))]
        args = list(x)
    else:
        in_specs, args = [row], [x]
    if has_resid:
        in_specs.append(row)
        args.append(o)
    in_specs += [pl.BlockSpec((None, 8, D), lambda i: (_seg_of_block(i, tm), 0, 0)),
                 pl.BlockSpec((1, D), lambda i: (0, 0))]
    args += [mods, g.reshape(1, D)]
    if router is not None:
        w_router, b_router, l = router
        in_specs += [pl.BlockSpec((None, D, N_EXPERTS), lambda i: (l, 0, 0)),
                     pl.BlockSpec((None, 1, N_EXPERTS), lambda i: (l, 0, 0))]
        args += [w_router, b_router.reshape(DEPTH, 1, N_EXPERTS)]
    out_specs, out_shape = [], []
    if has_resid:
        out_specs.append(row)
        out_shape.append(jax.ShapeDtypeStruct((T, D), F32))
    if rows3d:
        out_specs.append(pl.BlockSpec((tm * PACK_ROWS, LANE), lambda i: (i, 0)))
        out_shape.append(jax.ShapeDtypeStruct((T * PACK_ROWS, LANE), F32))
    else:
        out_specs.append(row)
        out_shape.append(jax.ShapeDtypeStruct((T, D), h_dtype))
    if router is not None:
        out_specs.append(pl.BlockSpec((tm, N_EXPERTS), lambda i: (i, 0)))
        out_shape.append(jax.ShapeDtypeStruct((T, N_EXPERTS), F32))
    return pl.pallas_call(
        functools.partial(_rmod_kernel, has_resid=has_resid, gate_row=gate_row, mode=mode,
                          mod_row=mod_row, emit_logits=router is not None, rows3d=rows3d, split_x=split_x),
        grid=(T // tm,), in_specs=in_specs, out_specs=out_specs, out_shape=out_shape,
        compiler_params=_cparams("arbitrary"),
        name="rmod",
    )(*args)


def _hgrn_consts(rev):
    L = UNIT
    t = np.arange(L)[:, None]
    s = np.arange(L)[None, :]
    order = (s >= t) if rev else (s <= t)
    masks = [((t // HGRN_LEVELS[0]) == (s // HGRN_LEVELS[0])) & order]
    masks += [(t // b) == (s // b) for b in HGRN_LEVELS[1:]]
    return jnp.asarray(order, BF16), jnp.asarray(np.stack(masks), F32)


def _hgrn_kernel(q_ref, v_ref, f_ref, lb_ref, s0_ref, tri_ref, msk_ref, o_ref, so_ref, st, *, rev):
    L = UNIT
    n = pl.program_id(1)
    is_ctx = n < BATCH

    @pl.when(is_ctx)
    def _():
        st[...] = jnp.zeros_like(st)

    @pl.when(jnp.logical_and(n >= BATCH, (n - BATCH) % (DEC_SEQ // L) == 0))
    def _():
        st[...] = s0_ref[...].T

    q = _silu(q_ref[...])
    v = v_ref[...].astype(BF16)
    lb = lb_ref[...]
    f = lb + (1.0 - lb) * _sigmoid(f_ref[...])
    k = 1.0 - f
    logf = jnp.log(f)
    hi = logf.astype(BF16)
    r1 = logf - hi.astype(F32)
    mid = r1.astype(BF16)
    lo = (r1 - mid.astype(F32)).astype(BF16)
    tri = tri_ref[...]
    bcum = _dot(tri, hi) + _dot(tri, mid) + _dot(tri, lo)

    rowi = lax.broadcasted_iota(jnp.int32, (L, A_DK), 0)
    scores = jnp.zeros((L, L), F32)
    for li, b in enumerate(HGRN_LEVELS):
        r = b // 2 if (rev or li == 0) else b // 2 - 1
        b3 = bcum.reshape(L // b, b, A_DK)
        ref = jnp.broadcast_to(b3[:, r:r + 1, :], (L // b, b, A_DK)).reshape(L, A_DK)
        dlt = bcum - ref
        if li == 0:
            qt = q * jnp.exp(jnp.clip(dlt, -HGRN_CLAMP, HGRN_CLAMP))
            kt = k * jnp.exp(jnp.clip(-dlt, -HGRN_CLAMP, HGRN_CLAMP))
        else:
            later = (rowi % b) >= (b // 2)
            q_rows = jnp.logical_not(later) if rev else later
            e = jnp.exp(-jnp.abs(dlt))
            qt = jnp.where(q_rows, q * e, 0.0)
            kt = jnp.where(q_rows, 0.0, k * e)
        scores = scores + msk_ref[li] * _dot_nt(qt.astype(BF16), kt.astype(BF16))

    s_prev = st[...]
    o = _dot(scores.astype(BF16), v) + _dot_nt((q * jnp.exp(bcum)).astype(BF16), s_prev.astype(BF16))
    o_ref[...] = o
    b_end = bcum[0:1, :] if rev else bcum[L - 1:L, :]
    kk = (k * jnp.exp(b_end - bcum)).astype(BF16)
    s_new = s_prev * jnp.exp(b_end) + _dot_tn(v, kk)
    st[...] = s_new

    @pl.when(is_ctx)
    def _():
        so_ref[...] = s_new.T


def _hgrn_dir(y0, lb, s0, *, rev):
    L = UNIT
    per = DEC_SEQ // L
    nb = A_QK // LANE
    d = 1 if rev else 0
    tri, msk = _hgrn_consts(rev)

    def rb(n):
        if not rev:
            return n
        m = n - BATCH
        return jnp.where(n < BATCH, n, BATCH + (m // per) * per + (per - 1 - m % per))

    def s0_idx(h, n):
        b = jnp.clip((n - BATCH) // per, 0, DEC_BATCH - 1)
        return (b * 2 * A_HEADS + d * A_HEADS + h, 0, 0)

    o, so = pl.pallas_call(
        functools.partial(_hgrn_kernel, rev=rev),
        grid=(A_HEADS, T // L),
        in_specs=[pl.BlockSpec((L, LANE), lambda h, n: (rb(n), h)),
                  pl.BlockSpec((L, LANE), lambda h, n: (rb(n), nb + h)),
                  pl.BlockSpec((L, LANE), lambda h, n: (rb(n), (2 + d) * nb + h)),
                  pl.BlockSpec((1, LANE), lambda h, n: (0, h)),
                  pl.BlockSpec((None, A_DK, A_DK), s0_idx),
                  pl.BlockSpec((L, L), lambda h, n: (0, 0)),
                  pl.BlockSpec((len(HGRN_LEVELS), L, L), lambda h, n: (0, 0, 0))],
        out_specs=[pl.BlockSpec((L, LANE), lambda h, n: (rb(n), h)),
                   pl.BlockSpec((None, A_DK, A_DK),
                                lambda h, n: (jnp.minimum(n, BATCH - 1) * A_HEADS + h, 0, 0))],
        out_shape=[jax.ShapeDtypeStruct((T, A_QK), F32),
                   jax.ShapeDtypeStruct((BATCH * A_HEADS, A_DK, A_DK), F32)],
        scratch_shapes=[pltpu.VMEM((A_DK, A_DK), F32)],
        compiler_params=_cparams("arbitrary", "arbitrary"),
        name="hgrn_bwd" if rev else "hgrn_fwd",
    )(y0, y0, y0, lb, s0, tri, msk)
    return o, so.reshape(BATCH, A_HEADS, A_DK, A_DK)


def _hgrn_out_kernel(of_ref, ob_ref, g_ref, gn_ref, o_ref):
    o = _rms(of_ref[...] + ob_ref[...], gn_ref[...])
    o_ref[...] = (o * _silu(g_ref[...])).astype(o_ref.dtype)


def _hgrn_out(o_fw, o_bw, y0, g_out):
    tm = 1024
    nb = A_QK // LANE
    blk = lambda off: pl.BlockSpec((tm, LANE), lambda i, h: (i, off + h))
    return pl.pallas_call(
        _hgrn_out_kernel,
        grid=(T // tm, A_HEADS),
        in_specs=[blk(0), blk(0), blk(4 * nb), pl.BlockSpec((1, LANE), lambda i, h: (0, 0))],
        out_specs=blk(0),
        out_shape=jax.ShapeDtypeStruct((T, A_QK), BF16),
        compiler_params=_cparams("arbitrary", "arbitrary"),
        name="hgrn_out",
    )(o_fw, o_bw, y0, g_out.reshape(1, LANE))


CONV_HALO = 16
CONV_RC, CONV_CC = 32, 256


def _conv_kernel(ap_ref, gp_ref, a_ref, g_ref, an_ref, gn_ref, w_ref, b_ref, lg_ref, lb_ref,
                 o_ref, pad, sh, yb):
    L = UNIT
    SUB = 8
    span = L + CONV_HALO + SUB
    n = pl.program_id(0)
    s = (n - BATCH) % (DEC_SEQ // L)
    lat = n >= BATCH
    has_prev = jnp.logical_and(lat, s > 0).astype(F32)
    has_next = jnp.logical_and(lat, s < DEC_SEQ // L - 1).astype(F32)
    glu = lambda a, g: a[...] * _sigmoid(g[...])
    pad[0:CONV_HALO, :] = glu(ap_ref, gp_ref) * has_prev
    pad[CONV_HALO:CONV_HALO + L, :] = glu(a_ref, g_ref)
    pad[CONV_HALO + L:CONV_HALO + L + CONV_HALO, :] = glu(an_ref, gn_ref) * has_next
    base = CONV_HALO - CONV_K // 2
    for b in range(1, SUB):
        sh[b - 1, 0:span, :] = pad[b:b + span, :]
    for c0 in range(0, B_W, CONV_CC):
        w = w_ref[:, c0:c0 + CONV_CC]
        for r0 in range(0, L, CONV_RC):
            acc = jnp.zeros((CONV_RC, CONV_CC), F32)
            for kk in range(CONV_K):
                a, b = divmod(base + kk, SUB)
                lo = r0 + a * SUB
                if b == 0:
                    win = pad[lo:lo + CONV_RC, c0:c0 + CONV_CC]
                else:
                    win = sh[b - 1, lo:lo + CONV_RC, c0:c0 + CONV_CC]
                acc = acc + w[kk:kk + 1, :] * win
            yb[r0:r0 + CONV_RC, c0:c0 + CONV_CC] = acc
    y = yb[...] + b_ref[...]
    mu = jnp.mean(y, axis=-1, keepdims=True)
    yc = y - mu
    yn = yc * lax.rsqrt(jnp.mean(yc * yc, axis=-1, keepdims=True) + EPS) * lg_ref[...] + lb_ref[...]
    o_ref[...] = _silu(yn).astype(o_ref.dtype)


def _conv(y0, conv_w, conv_b, ln_g, ln_b):
    L = UNIT
    ca = (3 * A_QK + 2 * A_QK) // B_W
    hb = L // CONV_HALO
    nhalo = T // CONV_HALO
    prev = lambda off: pl.BlockSpec((CONV_HALO, B_W), lambda n: (jnp.maximum(n * hb - 1, 0), off))
    cur = lambda off: pl.BlockSpec((L, B_W), lambda n: (n, off))
    nxt = lambda off: pl.BlockSpec((CONV_HALO, B_W), lambda n: (jnp.minimum((n + 1) * hb, nhalo - 1), off))
    vec = pl.BlockSpec((1, B_W), lambda n: (0, 0))
    return pl.pallas_call(
        _conv_kernel,
        grid=(T // L,),
        in_specs=[prev(ca), prev(ca + 1), cur(ca), cur(ca + 1), nxt(ca), nxt(ca + 1),
                  pl.BlockSpec((CONV_K, B_W), lambda n: (0, 0)), vec, vec, vec],
        out_specs=pl.BlockSpec((L, B_W), lambda n: (n, 0)),
        out_shape=jax.ShapeDtypeStruct((T, B_W), BF16),
        scratch_shapes=[pltpu.VMEM((L + 2 * CONV_HALO, B_W), F32),
                        pltpu.VMEM((7, L + CONV_HALO + 8, B_W), F32), pltpu.VMEM((L, B_W), F32)],
        compiler_params=_cparams("arbitrary"),
        name="conv",
    )(y0, y0, y0, y0, y0, y0, conv_w, conv_b.reshape(1, B_W), ln_g.reshape(1, B_W), ln_b.reshape(1, B_W))


def _axial_tables(L, rot_dim):
    rows = L // GRID_W
    row = np.repeat(np.arange(rows), GRID_W).astype(np.float32)
    col = np.tile(np.arange(GRID_W), rows).astype(np.float32)
    quarter = rot_dim // 4
    inv = (ROPE_BASE ** (-np.arange(quarter, dtype=np.float32) / quarter)).astype(np.float32)
    ang = np.concatenate([row[:, None] * inv, col[:, None] * inv], axis=-1)
    return np.cos(ang).astype(np.float32), np.sin(ang).astype(np.float32)


def _rope_tables(L, rot_dim, reps):
    cos, sin = _axial_tables(L, rot_dim)
    c = np.tile(np.concatenate([cos, cos], axis=-1), (1, reps))
    s = np.tile(np.concatenate([-sin, sin], axis=-1), (1, reps))
    return jnp.asarray(c), jnp.asarray(s)


def _rope(x, c, s, rot_dim):
    w = x.shape[-1]
    half = rot_dim // 2
    if rot_dim == w:
        swapped = pltpu.roll(x, half, 1)
    else:
        lane = lax.broadcasted_iota(jnp.int32, x.shape, 1)
        swapped = jnp.where((lane % rot_dim) < half, pltpu.roll(x, w - half, 1), pltpu.roll(x, half, 1))
    return x * c + swapped * s


def _softmax_parts(scores):
    m = scores[0].max(axis=-1, keepdims=True)
    for s in scores[1:]:
        m = jnp.maximum(m, s.max(axis=-1, keepdims=True))
    es = [jnp.exp(s - m) for s in scores]
    den = es[0].sum(axis=-1, keepdims=True)
    for e in es[1:]:
        den = den + e.sum(axis=-1, keepdims=True)
    inv = 1.0 / den
    return [e * inv for e in es]


def _diff_kernel(*refs, two_seg, lam_init):
    it = iter(refs)
    lam_ref, q1_ref, q2_ref = next(it), next(it), next(it)
    if two_seg:
        k1c_ref, k2c_ref, vc_ref = next(it), next(it), next(it)
    k1_ref, k2_ref, v_ref = next(it), next(it), next(it)
    if two_seg:
        cq_ref, sq_ref, ck_ref, sk_ref = next(it), next(it), next(it), next(it)
    gs_ref, o_ref = next(it), next(it)
    scale = C_DH ** -0.5
    lam = lam_ref[0, 0]
    outs = []
    ps = []
    for q_ref, kc_ref, k_ref in ((q1_ref, k1c_ref if two_seg else None, k1_ref),
                                 (q2_ref, k2c_ref if two_seg else None, k2_ref)):
        q = q_ref[...]
        k = k_ref[...]
        if two_seg:
            sc = [_dot_nt(q.astype(BF16), kc_ref[...].astype(BF16)) * scale,
                  _dot_nt(_rope(q, cq_ref[...], sq_ref[...], C_DH).astype(BF16),
                          _rope(k, ck_ref[...], sk_ref[...], C_DH).astype(BF16)) * scale]
        else:
            sc = [_dot_nt(q.astype(BF16), k.astype(BF16)) * scale]
        ps.append(_softmax_parts(sc))
    vs = ([vc_ref[...]] if two_seg else []) + [v_ref[...]]
    o = None
    for p1, p2, vv in zip(ps[0], ps[1], vs):
        t = _dot((p1 - lam * p2).astype(BF16), vv.astype(BF16))
        o = t if o is None else o + t
    o_ref[...] = (_rms(o, gs_ref[...]) * (1.0 - lam_init)).astype(o_ref.dtype)


def _diff_attn(y1, lam, g_sub, lam_init, cache_k=None, cache_v=None):
    two_seg = cache_k is not None
    hw = 2 * C_DH
    smem = pl.BlockSpec(memory_space=pltpu.SMEM)
    if not two_seg:
        grid = (BATCH, C_HEADS)
        blk = lambda off, w: pl.BlockSpec((SEQ, w), lambda b, h: (b, off + h * (hw // w)))
        in_specs = [smem, blk(0, C_DH), blk(1, C_DH),
                    blk(C_W // C_DH, C_DH), blk(C_W // C_DH + 1, C_DH), blk(2 * C_W // hw, hw),
                    pl.BlockSpec((1, hw), lambda b, h: (0, 0))]
        args = [lam, y1, y1, y1, y1, y1, g_sub.reshape(1, hw)]
        out_specs = pl.BlockSpec((SEQ, hw), lambda b, h: (b, h))
        out_rows = T_CTX
        sem = ("arbitrary", "arbitrary")
    else:
        tq = 256
        nq = DEC_SEQ // tq
        r0 = T_CTX // DEC_SEQ
        grid = (DEC_BATCH, C_HEADS, nq)
        qblk = lambda off: pl.BlockSpec((tq, C_DH), lambda b, h, i: (T_CTX // tq + b * nq + i, off + 2 * h))
        kblk = lambda off, w: pl.BlockSpec((DEC_SEQ, w), lambda b, h, i: (r0 + b, off + h * (hw // w)))
        cblk = lambda off, w: pl.BlockSpec((PAST, w), lambda b, h, i: (b, off + h * (hw // w)))
        tq_blk = pl.BlockSpec((tq, C_DH), lambda b, h, i: (i, 0))
        tk_blk = pl.BlockSpec((DEC_SEQ, C_DH), lambda b, h, i: (0, 0))
        c, s = _rope_tables(DEC_SEQ, C_DH, 1)
        in_specs = [smem, qblk(0), qblk(1), cblk(0, C_DH), cblk(1, C_DH), cblk(0, hw),
                    kblk(C_W // C_DH, C_DH), kblk(C_W // C_DH + 1, C_DH), kblk(2 * C_W // hw, hw),
                    tq_blk, tq_blk, tk_blk, tk_blk,
                    pl.BlockSpec((1, hw), lambda b, h, i: (0, 0))]
        args = [lam, y1, y1, cache_k, cache_k, cache_v, y1, y1, y1, c, s, c, s, g_sub.reshape(1, hw)]
        out_specs = pl.BlockSpec((tq, hw), lambda b, h, i: (b * nq + i, h))
        out_rows = T_LAT
        sem = ("arbitrary", "arbitrary", "arbitrary")
    return pl.pallas_call(
        functools.partial(_diff_kernel, two_seg=two_seg, lam_init=lam_init),
        grid=grid, in_specs=in_specs, out_specs=out_specs,
        out_shape=jax.ShapeDtypeStruct((out_rows, C_W), BF16),
        compiler_params=_cparams(*sem),
        name="diff_lat" if two_seg else "diff_ctx",
    )(*args)


def _mla_kernel(*refs, two_seg):
    it = iter(refs)
    qn_ref, qr_ref = next(it), next(it)
    if two_seg:
        kvc_ref, krc_ref = next(it), next(it)
    kv_ref, kr_ref = next(it), next(it)
    if two_seg:
        cq_ref, sq_ref, ck_ref, sk_ref = next(it), next(it), next(it), next(it)
    o_ref = next(it)
    scale = (NOPE + ROPE) ** -0.5
    qr = qr_ref[...]
    kr = kr_ref[...]
    if two_seg:
        qr_rot = _rope(qr, cq_ref[...], sq_ref[...], ROPE)
        kr_rot = _rope(kr, ck_ref[...], sk_ref[...], ROPE)
        krc = krc_ref[...]
    lane = lax.broadcasted_iota(jnp.int32, (1, LANE), 1)
    for h in range(D_HEADS):
        half = (lane < ROPE) if h % 2 == 0 else (lane >= ROPE)
        pr = slice((h // 2) * LANE, (h // 2 + 1) * LANE)
        qn = qn_ref[:, h * NOPE:(h + 1) * NOPE]
        kn = kv_ref[:, h * 2 * NOPE:h * 2 * NOPE + NOPE]
        vv = kv_ref[:, h * 2 * NOPE + NOPE:(h + 1) * 2 * NOPE]
        cat = lambda a, b: jnp.concatenate([a.astype(BF16), b.astype(BF16)], axis=-1)
        if two_seg:
            knc = kvc_ref[:, h * 2 * NOPE:h * 2 * NOPE + NOPE]
            vc = kvc_ref[:, h * 2 * NOPE + NOPE:(h + 1) * 2 * NOPE]
            sc = [_dot_nt(cat(qn, qr[:, pr]), cat(knc, jnp.where(half, krc, 0.0))) * scale,
                  _dot_nt(cat(qn, qr_rot[:, pr]), cat(kn, jnp.where(half, kr_rot, 0.0))) * scale]
            p = _softmax_parts(sc)
            o = _dot(p[0].astype(BF16), vc) + _dot(p[1].astype(BF16), vv)
        else:
            sc = [_dot_nt(cat(qn, qr[:, pr]), cat(kn, jnp.where(half, kr, 0.0))) * scale]
            o = _dot(_softmax_parts(sc)[0].astype(BF16), vv)
        o_ref[:, h * V_DIM:(h + 1) * V_DIM] = o.astype(o_ref.dtype)


def _mla_attn(qm, kv, y1, kv_c=None, kr_c=None):
    two_seg = kv_c is not None
    qn_w, qr_w = D_HEADS * NOPE, D_HEADS * ROPE
    kr_col = (Q_RANK + KV_RANK) // LANE
    if not two_seg:
        grid = (BATCH,)
        in_specs = [pl.BlockSpec((SEQ, qn_w), lambda b: (b, 0)),
                    pl.BlockSpec((SEQ, qr_w), lambda b: (b, qn_w // qr_w)),
                    pl.BlockSpec((SEQ, 2 * qn_w), lambda b: (b, 0)),
                    pl.BlockSpec((SEQ, LANE), lambda b: (b, kr_col))]
        args = [qm, qm, kv, y1]
        out_specs = pl.BlockSpec((SEQ, qn_w), lambda b: (b, 0))
        out_rows = T_CTX
        sem = ("arbitrary",)
    else:
        tq = 256
        nq = DEC_SEQ // tq
        r0 = T_CTX // DEC_SEQ
        grid = (DEC_BATCH, nq)
        cq, sq = _rope_tables(DEC_SEQ, ROPE, qr_w // ROPE)
        ck, sk = _rope_tables(DEC_SEQ, ROPE, LANE // ROPE)
        in_specs = [pl.BlockSpec((tq, qn_w), lambda b, i: (T_CTX // tq + b * nq + i, 0)),
                    pl.BlockSpec((tq, qr_w), lambda b, i: (T_CTX // tq + b * nq + i, qn_w // qr_w)),
                    pl.BlockSpec((PAST, 2 * qn_w), lambda b, i: (b, 0)),
                    pl.BlockSpec((PAST, LANE), lambda b, i: (b, 0)),
                    pl.BlockSpec((DEC_SEQ, 2 * qn_w), lambda b, i: (r0 + b, 0)),
                    pl.BlockSpec((DEC_SEQ, LANE), lambda b, i: (r0 + b, kr_col)),
                    pl.BlockSpec((tq, qr_w), lambda b, i: (i, 0)),
                    pl.BlockSpec((tq, qr_w), lambda b, i: (i, 0)),
                    pl.BlockSpec((DEC_SEQ, LANE), lambda b, i: (0, 0)),
                    pl.BlockSpec((DEC_SEQ, LANE), lambda b, i: (0, 0))]
        args = [qm, qm, kv_c, kr_c, kv, y1, cq, sq, ck, sk]
        out_specs = pl.BlockSpec((tq, qn_w), lambda b, i: (b * nq + i, 0))
        out_rows = T_LAT
        sem = ("arbitrary", "arbitrary")
    return pl.pallas_call(
        functools.partial(_mla_kernel, two_seg=two_seg),
        grid=grid, in_specs=in_specs, out_specs=out_specs,
        out_shape=jax.ShapeDtypeStruct((out_rows, qn_w), BF16),
        compiler_params=_cparams(*sem),
        name="mla_lat" if two_seg else "mla_ctx",
    )(*args)


def _route_kernel(lg_ref, tril_ref, o_ref, cnt_ref, carry):
    tb = ROUTE_TB

    @pl.when(pl.program_id(0) == 0)
    def _():
        carry[...] = jnp.zeros_like(carry)

    l = lg_ref[...]
    lane = lax.broadcasted_iota(jnp.int32, (tb, N_EXPERTS), 1).astype(F32)
    vals, ids, sels = [], [], []
    for _ in range(TOP_K):
        m = l.max(axis=-1, keepdims=True)
        idx = jnp.min(jnp.where(l == m, lane, float(N_EXPERTS)), axis=-1, keepdims=True)
        sel = lane == idx
        vals.append(m)
        ids.append(idx)
        sels.append(sel)
        l = jnp.where(sel, -jnp.inf, l)
    es = [jnp.exp(v - vals[0]) for v in vals]
    den = es[0]
    for e in es[1:]:
        den = den + e
    inv = 1.0 / den
    picked = jnp.zeros((tb, N_EXPERTS), F32)
    for sel in sels:
        picked = picked + jnp.where(sel, 1.0, 0.0)
    base = carry[...] + _dot(tril_ref[...], picked.astype(BF16))
    carry[...] = carry[...] + jnp.sum(picked, axis=0, keepdims=True)
    cnt_ref[...] = carry[...]
    out_lane = lax.broadcasted_iota(jnp.int32, (tb, LANE), 1)
    out = jnp.zeros((tb, LANE), F32)
    for k in range(TOP_K):
        rank = jnp.sum(jnp.where(sels[k], base, 0.0), axis=-1, keepdims=True)
        out = jnp.where(out_lane == k, ids[k], out)
        out = jnp.where(out_lane == TOP_K + k, es[k] * inv, out)
        out = jnp.where(out_lane == 2 * TOP_K + k, rank, out)
    o_ref[...] = out


def _route(logits):
    tb = ROUTE_TB
    tril = jnp.asarray(np.tril(np.ones((tb, tb), np.float32), -1), BF16)
    packed, counts = pl.pallas_call(
        _route_kernel,
        grid=(T // tb,),
        in_specs=[pl.BlockSpec((tb, N_EXPERTS), lambda i: (i, 0)),
                  pl.BlockSpec((tb, tb), lambda i: (0, 0))],
        out_specs=[pl.BlockSpec((tb, LANE), lambda i: (i, 0)),
                   pl.BlockSpec((1, N_EXPERTS), lambda i: (0, 0))],
        out_shape=[jax.ShapeDtypeStruct((T, LANE), F32), jax.ShapeDtypeStruct((1, N_EXPERTS), F32)],
        scratch_shapes=[pltpu.VMEM((1, N_EXPERTS), F32)],
        compiler_params=_cparams("arbitrary"),
        name="moe_route",
    )(logits, tril)
    eid = packed[:, 0:TOP_K].astype(jnp.int32)
    gates = packed[:, TOP_K:2 * TOP_K]
    rank = packed[:, 2 * TOP_K:3 * TOP_K].astype(jnp.int32)
    return eid, gates, rank, counts[0].astype(jnp.int32)


def _smem_rows(a, tb):
    return a.reshape(T // tb, tb, TOP_K).transpose(0, 2, 1).reshape(T // tb, 1, TOP_K * tb)


def _dispatch_kernel(pstart_ref, padst_ref, padn_ref, nu_ref, eid_ref, rank_ref, h_ref, xs_ref, dest_ref,
                     zblk, sem, zsem):
    tb = DISPATCH_TB
    i = pl.program_id(0)

    def body(t, c):
        for k in range(TOP_K):
            d = pstart_ref[eid_ref[0, 0, k * tb + t]] + rank_ref[0, 0, k * tb + t]
            dest_ref[0, 0, k * tb + t] = d
            pltpu.make_async_copy(h_ref.at[t], xs_ref.at[d], sem).start(priority=k % 2)
        return c
    lax.fori_loop(0, tb, body, 0)

    def pad_rows(act):
        def per_e(e, c):
            n, d = padn_ref[e], padst_ref[e]
            p = MOE_TM // 2
            while p:
                @pl.when((n & p) != 0)
                def _(d=d, p=p):
                    act(pltpu.make_async_copy(zblk.at[pl.ds(0, p)], xs_ref.at[pl.ds(d, p)], zsem))
                d = d + (n & p)
                p //= 2
            return c
        lax.fori_loop(0, N_EXPERTS, per_e, 0)

    def pad_blocks(act):
        def per_b(b, c):
            act(pltpu.make_async_copy(zblk, xs_ref.at[pl.ds(pl.multiple_of(b * MOE_TM, MOE_TM), MOE_TM)], zsem))
            return c
        lax.fori_loop(nu_ref[0], MOE_BLOCKS, per_b, 0)

    @pl.when(i == 0)
    def _():
        zblk[...] = jnp.zeros_like(zblk)
        pad_rows(lambda cp: cp.start())
        pad_blocks(lambda cp: cp.start())
        pad_rows(lambda cp: cp.wait())
        pad_blocks(lambda cp: cp.wait())

    for k in range(TOP_K):
        pltpu.make_async_copy(h_ref, xs_ref.at[pl.ds(0, tb)], sem).wait()


def _dispatch(h3, eid, rank, counts):
    tb = DISPATCH_TB
    padded = (counts + MOE_TM - 1) // MOE_TM * MOE_TM
    pend = jnp.cumsum(padded)
    pstart = pend - padded
    n_used = (pend[-1:] // MOE_TM).astype(jnp.int32)
    blk_start = jnp.arange(MOE_BLOCKS, dtype=jnp.int32) * MOE_TM
    blk_expert = jnp.minimum(jnp.sum((pend[None, :] <= blk_start[:, None]).astype(jnp.int32), axis=1),
                             N_EXPERTS - 1).astype(jnp.int32)
    nxt_first = (pend // MOE_TM)[blk_expert].astype(jnp.int32)
    smem = lambda: pl.BlockSpec((1, 1, TOP_K * tb), lambda i, *_: (i, 0, 0), memory_space=pltpu.SMEM)
    xs, dest = pl.pallas_call(
        _dispatch_kernel,
        grid_spec=pltpu.PrefetchScalarGridSpec(
            num_scalar_prefetch=4, grid=(T // tb,),
            in_specs=[smem(), smem(), pl.BlockSpec((tb, PACK_ROWS, LANE), lambda i, *_: (i, 0, 0))],
            out_specs=[pl.BlockSpec(memory_space=pl.ANY), smem()],
            scratch_shapes=[pltpu.VMEM((MOE_TM, PACK_ROWS, LANE), F32),
                            pltpu.SemaphoreType.DMA, pltpu.SemaphoreType.DMA]),
        out_shape=[jax.ShapeDtypeStruct((MOE_ROWS, PACK_ROWS, LANE), F32),
                   jax.ShapeDtypeStruct((T // tb, 1, TOP_K * tb), jnp.int32)],
        compiler_params=_cparams("arbitrary"),
        name="moe_dispatch",
    )(pstart.astype(jnp.int32), (pstart + counts).astype(jnp.int32), (padded - counts).astype(jnp.int32),
      n_used, _smem_rows(eid, tb), _smem_rows(rank, tb), h3)
    return xs, dest, (blk_expert, n_used, nxt_first)


def _swiglu(gate, up):
    gate = jnp.minimum(gate, SWIGLU_LIMIT)
    up = jnp.clip(up, -SWIGLU_LIMIT, SWIGLU_LIMIT)
    return gate * _sigmoid(SWIGLU_ALPHA * gate) * (up + 1.0)


def _new_expert(be_ref, i):
    return jnp.logical_or(i == 0, be_ref[i] != be_ref[jnp.maximum(i - 1, 0)])


def _stream_weights(be_ref, nu_ref, nxt_ref, w_hbms, w_f32s, w_bfs, sems, *, l, tn):
    j, i = pl.program_id(0), pl.program_id(1)

    def copies(e, jj):
        col = pl.multiple_of(jj * tn, tn)
        return [pltpu.make_async_copy(w.at[l, e, :, pl.ds(col, tn)], buf, sems.at[k])
                for k, (w, buf) in enumerate(zip(w_hbms, w_f32s))]

    @pl.when(_new_expert(be_ref, i))
    def _():
        @pl.when(jnp.logical_and(i == 0, j == 0))
        def _():
            for cp in copies(be_ref[0], 0):
                cp.start()

        for cp in copies(be_ref[i], j):
            cp.wait()
        for buf, bf in zip(w_f32s, w_bfs):
            bf[...] = buf[...].astype(BF16)
        nf = nxt_ref[i]
        more = nf < nu_ref[0]

        @pl.when(more)
        def _():
            for cp in copies(be_ref[jnp.minimum(nf, MOE_BLOCKS - 1)], j):
                cp.start()

        @pl.when(jnp.logical_and(jnp.logical_not(more), j + 1 < pl.num_programs(0)))
        def _():
            for cp in copies(be_ref[0], j + 1):
                cp.start()


def _moe_up_kernel(be_ref, nu_ref, nxt_ref, x_ref, wg_hbm, wu_hbm, bg_ref, bu_ref, o_ref,
                   wg_f32, wu_f32, wg_bf, wu_bf, x_bf, sems, *, l, tn):
    i = pl.program_id(1)

    @pl.when(i < nu_ref[0])
    def _():
        _stream_weights(be_ref, nu_ref, nxt_ref, (wg_hbm, wu_hbm), (wg_f32, wu_f32), (wg_bf, wu_bf), sems,
                        l=l, tn=tn)
        _load_packed_rows(x_ref, x_bf, x_bf.shape[0])
        x = x_bf[...]
        gate = _dot(x, wg_bf[...]) + bg_ref[...]
        up = _dot(x, wu_bf[...]) + bu_ref[...]
        o_ref[...] = _swiglu(gate, up).astype(o_ref.dtype)

    @pl.when(i >= nu_ref[0])
    def _():
        o_ref[...] = jnp.zeros_like(o_ref)


def _moe_down_kernel(be_ref, nu_ref, nxt_ref, h_ref, wd_hbm, bd_ref, o_ref, wd_f32, wd_bf, sems, *, l, tn):
    i = pl.program_id(1)

    @pl.when(i < nu_ref[0])
    def _():
        _stream_weights(be_ref, nu_ref, nxt_ref, (wd_hbm,), (wd_f32,), (wd_bf,), sems, l=l, tn=tn)
        o_ref[...] = _dot(h_ref[...], wd_bf[...]) + bd_ref[...]

    @pl.when(i >= nu_ref[0])
    def _():
        o_ref[...] = jnp.zeros_like(o_ref)


def _moe_experts(xs, tables, l, w_gate, b_gate, w_up, b_up, w_down, b_down):
    tm = MOE_TM
    row = lambda j, i, be, nu, nx: jnp.minimum(i, nu[0] - 1)
    exp = lambda j, i, be, nu, nx: be[jnp.minimum(i, nu[0] - 1)]
    x3spec = pl.BlockSpec((tm * PACK_ROWS, LANE), lambda *a: (row(*a), 0))
    xspec = pl.BlockSpec((tm, D), lambda *a: (row(*a), 0))
    bspec = lambda tn: pl.BlockSpec((None, None, 1, tn), lambda j, i, be, nu, nx: (l, exp(j, i, be, nu, nx), 0, j))
    ospec = lambda tn: pl.BlockSpec((tm, tn), lambda j, i, be, nu, nx: (i, j))
    hbm = pl.BlockSpec(memory_space=pl.ANY)
    b4 = lambda b: b.reshape(DEPTH, N_EXPERTS, 1, -1)
    tn = MOE_TN_UP
    hmid = pl.pallas_call(
        functools.partial(_moe_up_kernel, l=l, tn=tn),
        grid_spec=pltpu.PrefetchScalarGridSpec(
            num_scalar_prefetch=3, grid=(D_FF // tn, MOE_BLOCKS),
            in_specs=[x3spec, hbm, hbm, bspec(tn), bspec(tn)],
            out_specs=ospec(tn),
            scratch_shapes=[pltpu.VMEM((D, tn), F32), pltpu.VMEM((D, tn), F32),
                            pltpu.VMEM((D, tn), BF16), pltpu.VMEM((D, tn), BF16),
                            pltpu.VMEM((tm, D), BF16), pltpu.SemaphoreType.DMA((2,))]),
        out_shape=jax.ShapeDtypeStruct((MOE_ROWS, D_FF), BF16),
        compiler_params=_cparams("arbitrary", "arbitrary"),
        name="moe_up",
    )(*tables, xs, w_gate, w_up, b4(b_gate), b4(b_up))
    tn = MOE_TN_DOWN
    return pl.pallas_call(
        functools.partial(_moe_down_kernel, l=l, tn=tn),
        grid_spec=pltpu.PrefetchScalarGridSpec(
            num_scalar_prefetch=3, grid=(D // tn, MOE_BLOCKS),
            in_specs=[xspec, hbm, bspec(tn)],
            out_specs=ospec(tn),
            scratch_shapes=[pltpu.VMEM((D_FF, tn), F32), pltpu.VMEM((D_FF, tn), BF16),
                            pltpu.SemaphoreType.DMA((1,))]),
        out_shape=jax.ShapeDtypeStruct((MOE_ROWS, D), F32),
        compiler_params=_cparams("arbitrary", "arbitrary"),
        name="moe_down",
    )(*tables, hmid, w_down, b4(b_down))


def _combine_kernel(dest_ref, destn_ref, y_ref, gates_ref, x_ref, mods_ref, g_ref, *rest,
                    tb, nblk, mode, mod_row, emit_x):
    if emit_x:
        x2_ref, h_ref, buf, sem = rest
    else:
        hc_ref, hl_ref, buf, sem = rest
    i = pl.program_id(0)
    per = DISPATCH_TB // tb

    def issue(dref, step, slot):
        off = (step % per) * tb

        def body(r, c):
            for k in range(TOP_K):
                d = dref[0, 0, k * DISPATCH_TB + off + r]
                pltpu.make_async_copy(y_ref.at[pl.ds(d, 1), :],
                                      buf.at[slot, k, pl.ds(r, 1), :], sem.at[slot]).start(priority=k % 2)
            return c
        lax.fori_loop(0, tb, body, 0)

    @pl.when(i == 0)
    def _():
        issue(dest_ref, i, 0)

    @pl.when(i + 1 < nblk)
    def _():
        issue(destn_ref, i + 1, (i + 1) % 2)

    slot = i % 2
    for k in range(TOP_K):
        pltpu.make_async_copy(y_ref.at[pl.ds(0, tb), :], buf.at[slot, k], sem.at[slot]).wait()
    gates = gates_ref[...]
    moe = gates[:, 0:1] * buf[slot, 0]
    for k in range(1, TOP_K):
        moe = moe + gates[:, k:k + 1] * buf[slot, k]
    x, h = _resid_norm(x_ref[...], moe, mods_ref[...], g_ref[...],
                       has_resid=True, gate_row=5, mode=mode, mod_row=mod_row)
    if emit_x:
        x2_ref[...] = x
        h_ref[...] = h.astype(h_ref.dtype)
    else:
        @pl.when(i < T_CTX // tb)
        def _():
            hc_ref[...] = h

        @pl.when(i >= T_CTX // tb)
        def _():
            hl_ref[...] = h


def _moe_combine(y, route, x, mods_l, g, *, mode, mods_next=None):
    tb = COMBINE_TB
    nblk = T // tb
    per = DISPATCH_TB // tb
    dest, gates = route
    emit_x = mode == "mod"
    if emit_x:
        mods = jnp.concatenate([mods_next[:, 0:2], mods_l[:, 2:]], axis=1)
    else:
        mods = mods_l
    row = pl.BlockSpec((tb, D), lambda i: (i, 0))
    nctx = T_CTX // tb
    table = lambda step: pl.BlockSpec((1, 1, TOP_K * DISPATCH_TB), lambda i: (step(i) // per, 0, 0),
                                      memory_space=pltpu.SMEM)
    if emit_x:
        out_specs = [row, row]
        out_shape = [jax.ShapeDtypeStruct((T, D), F32), jax.ShapeDtypeStruct((T, D), BF16)]
    else:
        out_specs = [pl.BlockSpec((tb, D), lambda i: (jnp.minimum(i, nctx - 1), 0)),
                     pl.BlockSpec((tb, D), lambda i: (jnp.maximum(i - nctx, 0), 0))]
        out_shape = [jax.ShapeDtypeStruct((T_CTX, D), F32), jax.ShapeDtypeStruct((T_LAT, D), F32)]
    return pl.pallas_call(
        functools.partial(_combine_kernel, tb=tb, nblk=nblk, mode=mode, mod_row=0, emit_x=emit_x),
        grid=(nblk,),
        in_specs=[table(lambda i: i), table(lambda i: jnp.minimum(i + 1, nblk - 1)),
                  pl.BlockSpec(memory_space=pl.ANY),
                  pl.BlockSpec((tb, TOP_K), lambda i: (i, 0)),
                  row,
                  pl.BlockSpec((None, 8, D), lambda i: (_seg_of_block(i, tb), 0, 0)),
                  pl.BlockSpec((1, D), lambda i: (0, 0))],
        out_specs=out_specs, out_shape=out_shape,
        scratch_shapes=[pltpu.VMEM((2, TOP_K, tb, D), F32), pltpu.SemaphoreType.DMA((2,))],
        compiler_params=_cparams("arbitrary"),
        name="moe_combine",
    )(dest, dest, y, gates, x, mods, g.reshape(1, D))


def _moe(h3, logits, l, w_gate, b_gate, w_up, b_up, w_down, b_down):
    eid, gates, rank, counts = _route(logits)
    xs, dest, tables = _dispatch(h3.reshape(T, PACK_ROWS, LANE), eid, rank, counts)
    y = _moe_experts(xs.reshape(MOE_ROWS * PACK_ROWS, LANE), tables, l,
                     w_gate, b_gate, w_up, b_up, w_down, b_down)
    return y, (dest, gates)


def kernel(x_prompt, x_sample, state_hgrn, cache_diff_k, cache_diff_v, cache_mla_ckv, cache_mla_kr, c, c_ctx,
           norm_mix, norm_ffn, w_ada, b_ada, hgrn_lb, w_in_even, g_hgrn_out, conv_w, conv_b, conv_ln_g, conv_ln_b,
           w_out_even, w_in_odd, w_uq, w_ukv, g_q, g_kv, diff_lambda, g_sub, w_out_odd,
           w_router, b_router, w_gate, b_gate, w_up, b_up, w_down, b_down, norm_final):
    x = (x_prompt.reshape(T_CTX, D), x_sample.reshape(T_LAT, D))
    cvec = jnp.concatenate([c_ctx[None, :], c, jnp.zeros((8 - 1 - DEC_BATCH, D), F32)], axis=0)
    ada = _ada(cvec, w_ada, b_ada)
    mods = [jnp.pad(ada[l, :1 + DEC_BATCH].reshape(1 + DEC_BATCH, 6, D), ((0, 0), (0, 2), (0, 0)))
            for l in range(DEPTH)]
    lower_bounds = jnp.cumsum(jax.nn.softmax(hgrn_lb.astype(F32), axis=0), axis=0)
    moe_w = lambda: (w_gate, b_gate, w_up, b_up, w_down, b_down)

    (h,) = _rmod(x, None, mods[0], norm_mix[0], mode="mod", mod_row=0)
    y0 = _mm(h, w_in_even[0], name="in_even")
    s0 = state_hgrn[:, 0].reshape(DEC_BATCH * 2 * A_HEADS, A_DK, A_DK)
    lb = lower_bounds[0].reshape(1, A_QK)
    o_fw, s_fw = _hgrn_dir(y0, lb, s0, rev=False)
    o_bw, s_bw = _hgrn_dir(y0, lb, s0, rev=True)
    o_a = _hgrn_out(o_fw, o_bw, y0, g_hgrn_out[0])
    o_b = _conv(y0, conv_w[0], conv_b[0], conv_ln_g[0], conv_ln_b[0])
    o = _mm([o_a, o_b], w_out_even[0], name="out_even")
    x, h3, logits = _rmod(x, o, mods[0], norm_ffn[0], gate_row=2, mode="mod", mod_row=3, rows3d=True,
                          router=(w_router, b_router, 0))
    y, route = _moe(h3, logits, 0, *moe_w())
    x, h = _moe_combine(y, route, x, mods[0], norm_mix[1], mode="mod", mods_next=mods[1])
    new_hgrn = jnp.stack([s_fw, s_bw], axis=1)[:, None]

    y1 = _mm(h, w_in_odd[0], n_cols=3 * C_W, name="in_odd")
    w_tail = jnp.concatenate([w_in_odd[0][:, 3 * C_W:], w_in_odd[0][:, ODD_IN - ROPE:]], axis=1)
    y1t = _mm(h, w_tail, name="in_odd_tail")
    perm = np.concatenate([np.arange(D_HEADS)[:, None] * (NOPE + ROPE) + np.arange(NOPE)[None, :],
                           np.arange(D_HEADS)[:, None] * (NOPE + ROPE) + NOPE + np.arange(ROPE)[None, :]],
                          axis=None)
    qm = _mm((y1t, Q_RANK, 0), w_uq[0][:, perm], g=g_q[0], tm=1024, tn=D_HEADS * (NOPE + ROPE), name="uq")
    kv, ckv = _mm((y1t, KV_RANK, Q_RANK // KV_RANK), w_ukv[0], g=g_kv[0], emit_norm=True, out_dtype=BF16,
                  tm=1024, tn=2048, name="ukv")
    kv_c = _mm(cache_mla_ckv[:, 0].reshape(DEC_BATCH * PAST, KV_RANK), w_ukv[0], out_dtype=BF16,
               tm=1024, tn=2048, name="ukv_cache")
    kr_c = cache_mla_kr[:, 0].reshape(DEC_BATCH * PAST, ROPE)
    kr_c = jnp.concatenate([kr_c, kr_c], axis=-1)
    lq = diff_lambda[0].astype(F32)
    lam_init = 0.8 - 0.6 * math.exp(-0.3 * 1)
    lam = (jnp.exp(jnp.sum(lq[0] * lq[1])) - jnp.exp(jnp.sum(lq[2] * lq[3])) + lam_init).reshape(1, 1)
    ck = cache_diff_k[:, 0].reshape(DEC_BATCH * PAST, C_W)
    cv = cache_diff_v[:, 0].reshape(DEC_BATCH * PAST, C_W)
    o_c = jnp.concatenate([_diff_attn(y1, lam, g_sub[0], lam_init),
                           _diff_attn(y1, lam, g_sub[0], lam_init, ck, cv)], axis=0)
    o_d = jnp.concatenate([_mla_attn(qm, kv, y1t), _mla_attn(qm, kv, y1t, kv_c, kr_c)], axis=0)
    o = _mm([o_c, o_d], w_out_odd[0], name="out_odd")
    x, h3, logits = _rmod(x, o, mods[1], norm_ffn[1], gate_row=2, mode="mod", mod_row=3, rows3d=True,
                          router=(w_router, b_router, 1))
    y, route = _moe(h3, logits, 1, *moe_w())
    y_ctx, y_lat = _moe_combine(y, route, x, mods[1], norm_final, mode="final")

    ctx = lambda a, shape: a[:T_CTX].reshape(shape)
    kr0 = Q_RANK + KV_RANK
    return (y_ctx.reshape(BATCH, SEQ, D), y_lat.reshape(DEC_BATCH, DEC_SEQ, D),
            new_hgrn,
            ctx(y1[:, C_W:2 * C_W], (BATCH, 1, SEQ, C_HEADS, 2 * C_DH)),
            ctx(y1[:, 2 * C_W:3 * C_W], (BATCH, 1, SEQ, C_HEADS, 2 * C_DH)),
            ctx(ckv, (BATCH, 1, SEQ, KV_RANK)),
            ctx(y1t[:, kr0:kr0 + ROPE], (BATCH, 1, SEQ, ROPE)))
```
